```python
import math
import jax
import jax.numpy as jnp
from jax import lax
import numpy as np

D_MODEL = 2048
BATCH = 2
SEQ = 8192
DEPTH = 4

N_MEM = 256
N_BRANCHES = 4
BRANCH_WIDTH = D_MODEL // 2
POOL_WIDTH = BRANCH_WIDTH
POOL_WINDOWS = (2, 4, 8, 16)
N_POOL_GROUPS = len(POOL_WINDOWS)
POOL_GROUP_DIM = POOL_WIDTH // N_POOL_GROUPS
RET_WIDTH = BRANCH_WIDTH
N_RET_HEADS = 8
RET_HEAD_DIM = RET_WIDTH // N_RET_HEADS
RET_CHUNK = 128
ROPE_BASE = 10000.0
SGU_WIDTH = BRANCH_WIDTH
SGU_CHUNK = 128
N_SGU_GROUPS = 4
SGU_GROUP_DIM = SGU_WIDTH // N_SGU_GROUPS
MEM_WIDTH = BRANCH_WIDTH
N_MEM_HEADS = 4
MEM_HEAD_DIM = MEM_WIDTH // N_MEM_HEADS
IN_SPLITS = (POOL_WIDTH, RET_WIDTH, RET_WIDTH, RET_WIDTH, RET_WIDTH, SGU_WIDTH, SGU_WIDTH, MEM_WIDTH)
IN_WIDTH = sum(IN_SPLITS)
N_EXPERTS = 32
TOP_K = 4
EXPERT_DIM = 3 * D_MODEL // 8
SWIGLU_LIMIT = 7.0
SWIGLU_ALPHA = 1.702
MOE_BLOCK = 256
LN_EPS = 1e-5
DEEPNORM_ALPHA = (2 * DEPTH) ** 0.25
DEEPNORM_BETA = (8 * DEPTH) ** -0.25

kernel_name = 'hybrid_pool_retention_sgu_memory_moe_encoder'


def layer_norm(x, g, b):
    xf = x.astype(jnp.float32)
    mu = jnp.mean(xf, axis=-1, keepdims=True)
    var = jnp.mean(jnp.square(xf - mu), axis=-1, keepdims=True)
    return ((xf - mu) * lax.rsqrt(var + LN_EPS) * g.astype(jnp.float32) + b.astype(jnp.float32)).astype(x.dtype)


def head_norm(y, g):
    B, S, H, d = y.shape
    yf = y.astype(jnp.float32)
    mu = jnp.mean(yf, axis=-1, keepdims=True)
    var = jnp.mean(jnp.square(yf - mu), axis=-1, keepdims=True)
    yn = ((yf - mu) * lax.rsqrt(var + LN_EPS)).reshape(B, S, H * d)
    return (yn * g.astype(jnp.float32)).astype(y.dtype)


def rotary_tables(seq, dim, dtype):
    pos = jnp.arange(seq, dtype=jnp.float32)
    inv_freq = jnp.exp(-math.log(ROPE_BASE) * jnp.arange(0, dim, 2, dtype=jnp.float32) / dim)
    ang = pos[:, None] * inv_freq[None, :]
    return jnp.cos(ang).astype(dtype), jnp.sin(ang).astype(dtype)


def rotary(x, cos, sin):
    x1, x2 = jnp.split(x, 2, axis=-1)
    c = cos[None, :, None, :]
    s = sin[None, :, None, :]
    return jnp.concatenate([x1 * c - x2 * s, x2 * c + x1 * s], axis=-1)


def pool_mixer(a, pool_w, pool_scale):
    B, S, _ = a.shape
    ag = a.reshape(B, S, N_POOL_GROUPS, POOL_GROUP_DIM)
    cs = jnp.cumsum(ag.astype(jnp.float32), axis=1)
    cs = jnp.pad(cs, ((0, 0), (1, 0), (0, 0), (0, 0)))
    pos = jnp.arange(S)
    pooled = []
    for gi, w in enumerate(POOL_WINDOWS):
        lo = jnp.clip(pos - w // 2, 0, S)
        hi = jnp.clip(pos + w // 2, 0, S)
        csg = cs[:, :, gi]
        cnt = (hi - lo).astype(jnp.float32)[None, :, None]
        pooled.append((csg[:, hi] - csg[:, lo]) / cnt)
    pooled = jnp.stack(pooled, axis=2)
    mixed = (pooled - ag.astype(jnp.float32)).astype(a.dtype)
    y = jnp.einsum('bsgc,gce->bsge', mixed, pool_w)
    return y.reshape(B, S, POOL_WIDTH) * pool_scale


def retention_scan(q, k, v, log_gamma, include_diag):
    B, H, S, d = q.shape
    C = RET_CHUNK
    n = S // C
    f32 = jnp.float32
    qc, kc, vc = (t.reshape(B, H, n, C, d).astype(f32) for t in (q, k, v))
    lg = log_gamma.astype(f32)
    idx = jnp.arange(C, dtype=f32)
    diff = idx[:, None] - idx[None, :]
    mask = (diff >= 0) if include_diag else (diff > 0)
    decay = jnp.where(mask[None], jnp.exp(lg[:, None, None] * jnp.maximum(diff, 0.0)[None]), 0.0)
    scores = jnp.einsum('bhncd,bhnmd->bhncm', qc, kc) * decay[None, :, None]
    intra = jnp.einsum('bhncm,bhnme->bhnce', scores, vc)
    k_w = jnp.exp(lg[:, None] * (C - 1.0 - idx)[None])
    q_w = jnp.exp(lg[:, None] * (idx + 1.0)[None])
    chunk_kv = jnp.einsum('bhncd,bhnce->nbhde', kc * k_w[None, :, None, :, None], vc)
    chunk_decay = jnp.exp(lg * C)[None, :, None, None]

    def step(state, kv):
        return state * chunk_decay + kv, state

    _, prev = lax.scan(step, jnp.zeros((B, H, d, d), f32), chunk_kv)
    cross = jnp.einsum('bhncd,nbhde->bhnce', qc, prev) * q_w[None, :, None, :, None]
    return (intra + cross).reshape(B, H, S, d).astype(q.dtype)


def retention_branch(q, k, v, g, log_gamma, gn_g, cos, sin):
    B, S, _ = q.shape
    H, d = N_RET_HEADS, RET_HEAD_DIM
    qh = rotary(q.reshape(B, S, H, d), cos, sin)
    kh = rotary(k.reshape(B, S, H, d), cos, sin) * (d ** -0.5)
    vh = v.reshape(B, S, H, d)
    qh, kh, vh = (jnp.swapaxes(t, 1, 2) for t in (qh, kh, vh))
    fwd = retention_scan(qh, kh, vh, log_gamma[0], True)
    rq, rk, rv = (jnp.flip(t, axis=2) for t in (qh, kh, vh))
    bwd = jnp.flip(retention_scan(rq, rk, rv, log_gamma[1], False), axis=2)
    y = jnp.swapaxes(fwd + bwd, 1, 2)
    return head_norm(y, gn_g) * jax.nn.silu(g)


def spatial_gating(u, v, ln_g, ln_b, w_s, b_s):
    B, S, _ = v.shape
    vn = layer_norm(v, ln_g, ln_b)
    vg = vn.reshape(B, S // SGU_CHUNK, SGU_CHUNK, N_SGU_GROUPS, SGU_GROUP_DIM)
    mixed = jnp.einsum('gpq,bnqgc->bnpgc', w_s, vg) + jnp.swapaxes(b_s, 0, 1)[:, :, None]
    return u * mixed.reshape(B, S, SGU_WIDTH)


def memory_attention(q, mem, w_mem_kv):
    B, S, _ = q.shape
    M = mem.shape[1]
    k, v = jnp.split(jnp.dot(mem, w_mem_kv), 2, axis=-1)
    qh = q.reshape(B, S, N_MEM_HEADS, MEM_HEAD_DIM)
    kh = k.reshape(B, M, N_MEM_HEADS, MEM_HEAD_DIM)
    vh = v.reshape(B, M, N_MEM_HEADS, MEM_HEAD_DIM)
    s = jnp.einsum('bshd,bmhd->bhsm', qh, kh).astype(jnp.float32) * (MEM_HEAD_DIM ** -0.5)
    p = jax.nn.softmax(s, axis=-1).astype(vh.dtype)
    o = jnp.einsum('bhsm,bmhd->bshd', p, vh)
    return o.reshape(B, S, MEM_WIDTH)


def mixer_sublayer(h, mem, w_in, pool_w, pool_scale, ret_log_gamma, ret_gn_g, sgu_ln_g, sgu_ln_b,
                   sgu_w, sgu_b, w_mem_kv, w_branch, w_gate, b_gate, w_out, cos, sin):
    z = jnp.dot(h, w_in)
    cuts = np.cumsum(IN_SPLITS)[:-1].tolist()
    a, rq, rk, rv, rg, su, sv, mq = jnp.split(z, cuts, axis=-1)
    branches = (
        pool_mixer(a, pool_w, pool_scale),
        retention_branch(rq, rk, rv, rg, ret_log_gamma, ret_gn_g, cos, sin),
        spatial_gating(jax.nn.gelu(su), jax.nn.gelu(sv), sgu_ln_g, sgu_ln_b, sgu_w, sgu_b),
        memory_attention(mq, mem, w_mem_kv),
    )
    merged = None
    for bi, y in enumerate(branches):
        gate = jax.nn.sigmoid(jnp.dot(h, w_gate[bi]) + b_gate[bi])
        term = gate * jnp.dot(y, w_branch[bi])
        merged = term if merged is None else merged + term
    return jnp.dot(merged, w_out)


def clamped_swiglu(gu):
    x_glu, x_lin = jnp.split(gu, 2, axis=-1)
    x_glu = jnp.minimum(x_glu, SWIGLU_LIMIT)
    x_lin = jnp.clip(x_lin, -SWIGLU_LIMIT, SWIGLU_LIMIT)
    return x_glu * jax.nn.sigmoid(SWIGLU_ALPHA * x_glu) * (x_lin + 1.0)


def moe_ffn(h, w_router, b_router, w_up, b_up, w_down, b_down):
    T, D = h.shape
    A = T * TOP_K
    n_blocks = -(-(A + N_EXPERTS * (MOE_BLOCK - 1)) // MOE_BLOCK)
    logits = jnp.dot(h, w_router).astype(jnp.float32) + b_router.astype(jnp.float32)
    top_logits, top_e = lax.top_k(logits, TOP_K)
    gates = jax.nn.softmax(top_logits, axis=-1)
    flat_e = top_e.reshape(A)
    flat_tok = jnp.repeat(jnp.arange(T, dtype=jnp.int32), TOP_K)
    flat_gate = gates.reshape(A)
    order = jnp.argsort(flat_e)
    sorted_e = flat_e[order]
    counts = jnp.bincount(flat_e, length=N_EXPERTS)
    padded = (counts + MOE_BLOCK - 1) // MOE_BLOCK * MOE_BLOCK
    pad_end = jnp.cumsum(padded)
    pad_start = pad_end - padded
    start = jnp.cumsum(counts) - counts
    dest = pad_start[sorted_e] + jnp.arange(A, dtype=jnp.int32) - start[sorted_e]
    n_slots = n_blocks * MOE_BLOCK
    slot_tok = jnp.full((n_slots,), T, jnp.int32).at[dest].set(flat_tok[order])
    slot_gate = jnp.zeros((n_slots,), jnp.float32).at[dest].set(flat_gate[order])
    block_e = jnp.minimum(jnp.searchsorted(pad_end, jnp.arange(n_blocks, dtype=jnp.int32) * MOE_BLOCK, side='right'),
                          N_EXPERTS - 1)
    h_pad = jnp.concatenate([h, jnp.zeros((1, D), h.dtype)], axis=0)

    def block_step(acc, blk):
        tok, gate, e = blk
        gu = jnp.dot(h_pad[tok], w_up[e]) + b_up[e]
        y = jnp.dot(clamped_swiglu(gu), w_down[e]) + b_down[e]
        return acc.at[tok].add(y.astype(jnp.float32) * gate[:, None]), None

    acc, _ = lax.scan(block_step, jnp.zeros((T + 1, D), jnp.float32),
                      (slot_tok.reshape(n_blocks, MOE_BLOCK), slot_gate.reshape(n_blocks, MOE_BLOCK), block_e))
    return acc[:T].astype(h.dtype)


def setup_inputs(seed: int = 0) -> dict:
    key = jax.random.key(seed)
    ks = jax.random.split(key, 32)
    f32 = jnp.float32
    L, D = DEPTH, D_MODEL

    def nrm(k, shape, scale):
        return jax.random.normal(k, shape, f32) * scale

    base_log_gamma = jnp.log1p(-jnp.exp2(-5.0 - jnp.arange(N_RET_HEADS, dtype=f32)))
    w_mem_kv = jnp.concatenate([nrm(ks[13], (L, D, MEM_WIDTH), D ** -0.5),
                                nrm(ks[14], (L, D, MEM_WIDTH), D ** -0.5 * DEEPNORM_BETA)], axis=-1)
    return {
        'x': nrm(ks[0], (BATCH, SEQ, D), 1.0),
        'mem': nrm(ks[1], (BATCH, N_MEM, D), 1.0),
        'ln_in_g': 1.0 + nrm(ks[2], (D,), 0.02),
        'ln_in_b': nrm(ks[3], (D,), 0.02),
        'w_in': nrm(ks[4], (L, D, IN_WIDTH), D ** -0.5),
        'pool_w': nrm(ks[5], (L, N_POOL_GROUPS, POOL_GROUP_DIM, POOL_GROUP_DIM), POOL_GROUP_DIM ** -0.5),
        'pool_scale': 1.0 + nrm(ks[6], (L, POOL_WIDTH), 0.1),
        'ret_log_gamma': base_log_gamma[None, None, :] * jnp.exp(nrm(ks[7], (L, 2, N_RET_HEADS), 0.1)),
        'ret_gn_g': 1.0 + nrm(ks[8], (L, RET_WIDTH), 0.02),
        'sgu_ln_g': 1.0 + nrm(ks[9], (L, SGU_WIDTH), 0.02),
        'sgu_ln_b': nrm(ks[10], (L, SGU_WIDTH), 0.02),
        'sgu_w': nrm(ks[11], (L, N_SGU_GROUPS, SGU_CHUNK, SGU_CHUNK), 0.5 * SGU_CHUNK ** -0.5),
        'sgu_b': 1.0 + nrm(ks[12], (L, N_SGU_GROUPS, SGU_CHUNK), 0.1),
        'w_mem_kv': w_mem_kv,
        'w_branch': nrm(ks[15], (L, N_BRANCHES, BRANCH_WIDTH, D), BRANCH_WIDTH ** -0.5 * DEEPNORM_BETA),
        'w_gate': nrm(ks[16], (L, N_BRANCHES, D, D), D ** -0.5),
        'b_gate': nrm(ks[17], (L, N_BRANCHES, D), 0.02),
        'w_out': nrm(ks[18], (L, D, D), D ** -0.5 * DEEPNORM_BETA),
        'ln1_g': 1.0 + nrm(ks[19], (L, D), 0.02),
        'ln1_b': nrm(ks[20], (L, D), 0.02),
        'w_router': nrm(ks[21], (L, D, N_EXPERTS), D ** -0.5),
        'b_router': nrm(ks[22], (L, N_EXPERTS), 0.01),
        'w_up': nrm(ks[23], (L, N_EXPERTS, D, 2 * EXPERT_DIM), D ** -0.5),
        'b_up': nrm(ks[24], (L, N_EXPERTS, 2 * EXPERT_DIM), 0.02),
        'w_down': nrm(ks[25], (L, N_EXPERTS, EXPERT_DIM, D), EXPERT_DIM ** -0.5 * DEEPNORM_BETA),
        'b_down': nrm(ks[26], (L, N_EXPERTS, D), 0.02),
        'ln2_g': 1.0 + nrm(ks[27], (L, D), 0.02),
        'ln2_b': nrm(ks[28], (L, D), 0.02),
    }


def reference(x, mem, ln_in_g, ln_in_b, w_in, pool_w, pool_scale, ret_log_gamma, ret_gn_g, sgu_ln_g, sgu_ln_b,
              sgu_w, sgu_b, w_mem_kv, w_branch, w_gate, b_gate, w_out, ln1_g, ln1_b, w_router, b_router,
              w_up, b_up, w_down, b_down, ln2_g, ln2_b):
    B, S, D = x.shape
    cos, sin = rotary_tables(S, RET_HEAD_DIM, x.dtype)
    h = layer_norm(x, ln_in_g, ln_in_b)
    for l in range(DEPTH):
        mix = mixer_sublayer(h, mem, w_in[l], pool_w[l], pool_scale[l], ret_log_gamma[l], ret_gn_g[l],
                             sgu_ln_g[l], sgu_ln_b[l], sgu_w[l], sgu_b[l], w_mem_kv[l], w_branch[l],
                             w_gate[l], b_gate[l], w_out[l], cos, sin)
        h = layer_norm(DEEPNORM_ALPHA * h + mix, ln1_g[l], ln1_b[l])
        ffn = moe_ffn(h.reshape(B * S, D), w_router[l], b_router[l], w_up[l], b_up[l], w_down[l], b_down[l])
        h = layer_norm(DEEPNORM_ALPHA * h + ffn.reshape(B, S, D), ln2_g[l], ln2_b[l])
    return h
```

```python
import functools
import math

import jax
import jax.numpy as jnp
from jax import lax
from jax.experimental import pallas as pl
from jax.experimental.pallas import tpu as pltpu

F32 = jnp.float32
BF16 = jnp.bfloat16

N_BRANCHES = 4
POOL_WINDOWS = (2, 4, 8, 16)
N_RET_HEADS = 8
RET_CHUNK = 128
ROPE_BASE = 10000.0
SGU_CHUNK = 128
N_SGU_GROUPS = 4
N_MEM_HEADS = 4
N_EXPERTS = 32
TOP_K = 4
SWIGLU_LIMIT = 7.0
SWIGLU_ALPHA = 1.702
LN_EPS = 1e-5
N_IN_SPLITS = 8

V7X_VMEM_BYTES = 64 * 1024 * 1024
VMEM_LIMIT_BYTES = V7X_VMEM_BYTES - 8 * 1024 * 1024
BF16_SUBLANE_TILE = 16

MOE_BLOCK = 256


def _params(*semantics):
    return pltpu.CompilerParams(dimension_semantics=semantics, vmem_limit_bytes=VMEM_LIMIT_BYTES)


def _layer_norm(x, g, b):
    mu = jnp.mean(x, axis=-1, keepdims=True)
    xc = x - mu
    var = jnp.mean(xc * xc, axis=-1, keepdims=True)
    return xc * lax.rsqrt(var + LN_EPS) * g + b


def _ln_in_kernel(x_ref, g_ref, b_ref, h_ref, hb_ref):
    h = _layer_norm(x_ref[...], g_ref[...], b_ref[...])
    h_ref[...] = h
    hb_ref[...] = h.astype(BF16)


def _ln_in(x, g, b, tm=512):
    T, D = x.shape
    row = pl.BlockSpec((tm, D), lambda i: (i, 0))
    vec = pl.BlockSpec((1, D), lambda i: (0, 0))
    return pl.pallas_call(
        _ln_in_kernel, grid=(T // tm,), in_specs=[row, vec, vec], out_specs=[row, row],
        out_shape=[jax.ShapeDtypeStruct((T, D), F32), jax.ShapeDtypeStruct((T, D), BF16)],
        compiler_params=_params("parallel"), name="ln_in")(x, g.reshape(1, D), b.reshape(1, D))


def _matmul_kernel(a_ref, w_ref, o_ref):
    o_ref[...] = jnp.dot(a_ref[...], w_ref[...], preferred_element_type=F32).astype(o_ref.dtype)


def _matmul(a, w, tm, tn, name):
    M, K = a.shape
    N = w.shape[1]
    return pl.pallas_call(
        _matmul_kernel, grid=(N // tn, M // tm),
        in_specs=[pl.BlockSpec((tm, K), lambda j, i: (i, 0)), pl.BlockSpec((K, tn), lambda j, i: (0, j))],
        out_specs=pl.BlockSpec((tm, tn), lambda j, i: (i, j)),
        out_shape=jax.ShapeDtypeStruct((M, N), BF16),
        compiler_params=_params("parallel", "parallel"), name=name)(a, w)


def _pool_kernel(z_ref, prev_ref, next_ref, w_ref, scale_ref, o_ref, ext_ref, *, seq, ts):
    halo = BF16_SUBLANE_TILE
    gd = w_ref.shape[1]
    pos0 = (pl.program_id(0) % (seq // ts)) * ts
    ext_ref[0:halo, :] = jnp.where(pos0 == 0, 0.0, prev_ref[...].astype(F32))
    ext_ref[halo:halo + ts, :] = z_ref[...].astype(F32)
    ext_ref[halo + ts:, :] = jnp.where(pos0 + ts == seq, 0.0, next_ref[...].astype(F32))
    pos = pos0 + lax.broadcasted_iota(jnp.int32, (ts, gd), 0)
    for g, w in enumerate(POOL_WINDOWS):
        cols = slice(g * gd, (g + 1) * gd)
        half = w // 2
        acc = ext_ref[halo - half:halo - half + ts, cols]
        for d in range(-half + 1, half):
            acc = acc + ext_ref[halo + d:halo + d + ts, cols]
        cnt = jnp.minimum(pos + half, seq) - jnp.maximum(pos - half, 0)
        mixed = acc / cnt.astype(F32) - ext_ref[halo:halo + ts, cols]
        y = jnp.dot(mixed.astype(BF16), w_ref[g], preferred_element_type=F32)
        o_ref[:, cols] = (y * scale_ref[:, cols]).astype(o_ref.dtype)


def _pool(z, pool_w, pool_scale, seq, ts=512):
    T = z.shape[0]
    G, gd, _ = pool_w.shape
    width = G * gd
    halo = BF16_SUBLANE_TILE
    n_halo = T // halo
    kern = functools.partial(_pool_kernel, seq=seq, ts=ts)
    return pl.pallas_call(
        kern, grid=(T // ts,),
        in_specs=[
            pl.BlockSpec((ts, width), lambda i: (i, 0)),
            pl.BlockSpec((halo, width), lambda i: (jnp.maximum(i * (ts // halo) - 1, 0), 0)),
            pl.BlockSpec((halo, width), lambda i: (jnp.minimum((i + 1) * (ts // halo), n_halo - 1), 0)),
            pl.BlockSpec((G, gd, gd), lambda i: (0, 0, 0)),
            pl.BlockSpec((1, width), lambda i: (0, 0)),
        ],
        out_specs=pl.BlockSpec((ts, width), lambda i: (i, 0)),
        out_shape=jax.ShapeDtypeStruct((T, width), BF16),
        scratch_shapes=[pltpu.VMEM((ts + 2 * halo, width), F32)],
        compiler_params=_params("parallel"), name="pool")(z, z, z, pool_w, pool_scale.reshape(1, width))


def _retention_kernel(lg_ref, q_ref, k_ref, v_ref, g_ref, cos_ref, sin_ref, gn_ref, o_ref, rs_ref, *, seq):
    C = RET_CHUNK
    d = q_ref.shape[1]
    nc = seq // C
    head = pl.program_id(0) % N_RET_HEADS
    lgf = lg_ref[0, head]
    lgb = lg_ref[1, head]
    row = lax.broadcasted_iota(jnp.int32, (C, C), 0).astype(F32)
    col = lax.broadcasted_iota(jnp.int32, (C, C), 1).astype(F32)
    diff = row - col
    decay = jnp.where(diff >= 0, jnp.exp(lgf * jnp.maximum(diff, 0.0)), jnp.exp(lgb * jnp.maximum(-diff, 0.0)))
    rowd = lax.broadcasted_iota(jnp.int32, (C, d), 0).astype(F32)
    qwf = jnp.exp(lgf * (rowd + 1.0))
    qwb = jnp.exp(lgb * (C - rowd))
    kwf = jnp.exp(lgf * (C - 1.0 - rowd))
    kwb = jnp.exp(lgb * rowd)
    chunk_decay_f = jnp.exp(lgf * jnp.full((d, d), float(C), F32))
    chunk_decay_b = jnp.exp(lgb * jnp.full((d, d), float(C), F32))
    tn_dims = (((0,), (0,)), ((), ()))
    nt_dims = (((1,), (1,)), ((), ()))

    def rotary(ref, sl):
        x = ref[sl, :].astype(F32)
        return x * cos_ref[sl, :] + pltpu.roll(x, d // 2, axis=1) * sin_ref[sl, :]

    def chunk(n):
        return pl.ds(pl.multiple_of(n * C, C), C)

    def bwd_body(j, state):
        n = nc - 1 - j
        sl = chunk(n)
        rs_ref[n] = state.astype(BF16)
        k = rotary(k_ref, sl) * (d ** -0.5)
        kv = lax.dot_general((k * kwb).astype(BF16), v_ref[sl, :], tn_dims, preferred_element_type=F32)
        return state * chunk_decay_b + kv

    lax.fori_loop(0, nc, bwd_body, jnp.zeros((d, d), F32))

    def fwd_body(n, state):
        sl = chunk(n)
        q = rotary(q_ref, sl)
        k = rotary(k_ref, sl) * (d ** -0.5)
        qb = q.astype(BF16)
        v = v_ref[sl, :]
        scores = lax.dot_general(qb, k.astype(BF16), nt_dims, preferred_element_type=F32) * decay
        y = jnp.dot(scores.astype(BF16), v, preferred_element_type=F32)
        y = y + jnp.dot(qb, state.astype(BF16), preferred_element_type=F32) * qwf
        y = y + jnp.dot(qb, rs_ref[n], preferred_element_type=F32) * qwb
        mu = jnp.mean(y, axis=-1, keepdims=True)
        yc = y - mu
        var = jnp.mean(yc * yc, axis=-1, keepdims=True)
        yn = yc * lax.rsqrt(var + LN_EPS) * gn_ref[...]
        gate = g_ref[sl, :].astype(F32)
        o_ref[sl, :] = (yn * (gate * jax.nn.sigmoid(gate))).astype(o_ref.dtype)
        kv = lax.dot_general((k * kwf).astype(BF16), v, tn_dims, preferred_element_type=F32)
        return state * chunk_decay_f + kv

    lax.fori_loop(0, nc, fwd_body, jnp.zeros((d, d), F32))


def _retention(z, log_gamma, gn_g, cos2, sin2, batch, seq):
    T = z.shape[0]
    H = N_RET_HEADS
    d = gn_g.shape[0] // H
    nc = seq // RET_CHUNK

    def col(split):
        return pl.BlockSpec((seq, d), lambda i: (i // H, split * H + i % H))

    table = pl.BlockSpec((seq, d), lambda i: (0, 0))
    kern = functools.partial(_retention_kernel, seq=seq)
    return pl.pallas_call(
        kern, grid=(batch * H,),
        in_specs=[pl.BlockSpec(memory_space=pltpu.SMEM), col(1), col(2), col(3), col(4), table, table,
                  pl.BlockSpec((1, d), lambda i: (0, i % H))],
        out_specs=pl.BlockSpec((seq, d), lambda i: (i // H, i % H)),
        out_shape=jax.ShapeDtypeStruct((T, H * d), BF16),
        scratch_shapes=[pltpu.VMEM((nc, d, d), BF16)],
        compiler_params=_params("parallel"), name="retention")(
            log_gamma, z, z, z, z, cos2, sin2, gn_g.reshape(1, H * d))


def _sgu_kernel(u_ref, v_ref, lng_ref, lnb_ref, w_ref, bias_ref, o_ref, *, chunks):
    P = SGU_CHUNK
    gd = o_ref.shape[1] // N_SGU_GROUPS

    def body(c, _):
        rows = pl.ds(pl.multiple_of(c * P, P), P)
        vn = _layer_norm(jax.nn.gelu(v_ref[rows, :].astype(F32)), lng_ref[...], lnb_ref[...]).astype(BF16)
        for g in range(N_SGU_GROUPS):
            cols = slice(g * gd, (g + 1) * gd)
            mixed = jnp.dot(w_ref[g], vn[:, cols], preferred_element_type=F32) + bias_ref[:, cols]
            u = jax.nn.gelu(u_ref[rows, cols].astype(F32))
            o_ref[rows, cols] = (u * mixed).astype(o_ref.dtype)
        return 0

    lax.fori_loop(0, chunks, body, 0)


def _sgu(z, ln_g, ln_b, w_s, b_s, chunks=4):
    T = z.shape[0]
    width = ln_g.shape[0]
    G, P, _ = w_s.shape
    rows = chunks * P
    bias = jnp.repeat(b_s.T.astype(F32), width // G, axis=1)
    kern = functools.partial(_sgu_kernel, chunks=chunks)
    vec = pl.BlockSpec((1, width), lambda i: (0, 0))
    return pl.pallas_call(
        kern, grid=(T // rows,),
        in_specs=[pl.BlockSpec((rows, width), lambda i: (i, 5)), pl.BlockSpec((rows, width), lambda i: (i, 6)),
                  vec, vec, pl.BlockSpec((G, P, P), lambda i: (0, 0, 0)), pl.BlockSpec((P, width), lambda i: (0, 0))],
        out_specs=pl.BlockSpec((rows, width), lambda i: (i, 0)),
        out_shape=jax.ShapeDtypeStruct((T, width), BF16),
        compiler_params=_params("parallel"), name="sgu")(
            z, z, ln_g.reshape(1, width), ln_b.reshape(1, width), w_s, bias)


def _mem_attn_kernel(q_ref, kv_ref, o_ref):
    width = o_ref.shape[1]
    hd = width // N_MEM_HEADS
    nt_dims = (((1,), (1,)), ((), ()))
    for h in range(N_MEM_HEADS):
        cols = slice(h * hd, (h + 1) * hd)
        s = lax.dot_general(q_ref[:, cols], kv_ref[:, cols], nt_dims, preferred_element_type=F32) * (hd ** -0.5)
        p = jnp.exp(s - jnp.max(s, axis=-1, keepdims=True))
        p = p / jnp.sum(p, axis=-1, keepdims=True)
        v = kv_ref[:, width + h * hd:width + (h + 1) * hd]
        o_ref[:, cols] = jnp.dot(p.astype(BF16), v, preferred_element_type=F32).astype(o_ref.dtype)


def _mem_attn(z, kv, seq, n_mem, tm=512):
    T = z.shape[0]
    width = kv.shape[1] // 2
    return pl.pallas_call(
        _mem_attn_kernel, grid=(T // tm,),
        in_specs=[pl.BlockSpec((tm, width), lambda i: (i, 7)),
                  pl.BlockSpec((n_mem, 2 * width), lambda i: (i // (seq // tm), 0))],
        out_specs=pl.BlockSpec((tm, width), lambda i: (i, 0)),
        out_shape=jax.ShapeDtypeStruct((T, width), BF16),
        compiler_params=_params("parallel"), name="mem_attn")(z, kv)


def _gate_merge_kernel(hb_ref, y0_ref, y1_ref, y2_ref, y3_ref, wg_ref, bg_ref, wb_ref, o_ref):
    acc = None
    for b, y_ref in enumerate((y0_ref, y1_ref, y2_ref, y3_ref)):
        gate = jax.nn.sigmoid(jnp.dot(hb_ref[...], wg_ref[b], preferred_element_type=F32) + bg_ref[b:b + 1, :])
        term = gate * jnp.dot(y_ref[...], wb_ref[b], preferred_element_type=F32)
        acc = term if acc is None else acc + term
    o_ref[...] = acc.astype(o_ref.dtype)


def _gate_merge(hb, ys, w_gate, b_gate, w_branch, tm=512, tn=512):
    T, D = hb.shape
    bw = w_branch.shape[1]
    yspec = pl.BlockSpec((tm, bw), lambda j, i: (i, 0))
    return pl.pallas_call(
        _gate_merge_kernel, grid=(D // tn, T // tm),
        in_specs=[pl.BlockSpec((tm, D), lambda j, i: (i, 0)), yspec, yspec, yspec, yspec,
                  pl.BlockSpec((N_BRANCHES, D, tn), lambda j, i: (0, 0, j)),
                  pl.BlockSpec((N_BRANCHES, tn), lambda j, i: (0, j)),
                  pl.BlockSpec((N_BRANCHES, bw, tn), lambda j, i: (0, 0, j))],
        out_specs=pl.BlockSpec((tm, tn), lambda j, i: (i, j)),
        out_shape=jax.ShapeDtypeStruct((T, D), BF16),
        compiler_params=_params("parallel", "parallel"), name="gate_merge")(hb, *ys, w_gate, b_gate, w_branch)


def _out_ln_kernel(m_ref, w_ref, h_ref, g_ref, b_ref, ho_ref, hbo_ref, *, alpha):
    mix = jnp.dot(m_ref[...], w_ref[...], preferred_element_type=F32)
    hn = _layer_norm(alpha * h_ref[...] + mix, g_ref[...], b_ref[...])
    ho_ref[...] = hn
    hbo_ref[...] = hn.astype(BF16)


def _out_ln(merged, w_out, h, g, b, alpha, tm=256):
    T, D = h.shape
    row = pl.BlockSpec((tm, D), lambda i: (i, 0))
    vec = pl.BlockSpec((1, D), lambda i: (0, 0))
    kern = functools.partial(_out_ln_kernel, alpha=alpha)
    return pl.pallas_call(
        kern, grid=(T // tm,),
        in_specs=[row, pl.BlockSpec((D, D), lambda i: (0, 0)), row, vec, vec],
        out_specs=[row, row],
        out_shape=[jax.ShapeDtypeStruct((T, D), F32), jax.ShapeDtypeStruct((T, D), BF16)],
        compiler_params=_params("parallel"), name="out_ln1")(merged, w_out, h, g.reshape(1, D), b.reshape(1, D))


def _router_kernel(hb_ref, wr_ref, br_ref, e_ref, gate_ref, rank_ref, cnt_ref, carry_ref):
    tm = hb_ref.shape[0]

    @pl.when(pl.program_id(0) == 0)
    def _():
        carry_ref[...] = jnp.zeros_like(carry_ref)

    nt_dims = (((1,), (1,)), ((), ()))
    logits = lax.dot_general(wr_ref[...], hb_ref[...], nt_dims, preferred_element_type=F32) + br_ref[...]
    eidx = lax.broadcasted_iota(jnp.int32, logits.shape, 0)
    work = logits
    tops, hots = [], []
    for k in range(TOP_K):
        m = jnp.max(work, axis=0, keepdims=True)
        sel = jnp.min(jnp.where(work == m, eidx, N_EXPERTS), axis=0, keepdims=True)
        hot = eidx == sel
        work = jnp.where(hot, -jnp.inf, work)
        e_ref[k:k + 1, :] = sel
        tops.append(m)
        hots.append(hot)
    ex = [jnp.exp(t - tops[0]) for t in tops]
    den = ex[0] + ex[1] + ex[2] + ex[3]
    member = jnp.zeros(logits.shape, F32)
    for hot in hots:
        member = member + jnp.where(hot, 1.0, 0.0)
    tri = jnp.where(lax.broadcasted_iota(jnp.int32, (tm, tm), 0) <= lax.broadcasted_iota(jnp.int32, (tm, tm), 1),
                    1.0, 0.0).astype(BF16)
    incl = jnp.dot(member.astype(BF16), tri, preferred_element_type=F32)
    before = carry_ref[:, 0:1] + incl - member
    for k in range(TOP_K):
        gate_ref[k:k + 1, :] = ex[k] / den
        rank_ref[k:k + 1, :] = jnp.sum(jnp.where(hots[k], before, 0.0), axis=0, keepdims=True).astype(jnp.int32)
    carry_ref[...] = carry_ref[...] + incl[:, tm - 1:tm]
    cnt_ref[...] = carry_ref[...].astype(jnp.int32)


def _router(hb, w_router_t, b_router, tm=512):
    T, D = hb.shape
    E = w_router_t.shape[0]
    tok = pl.BlockSpec((TOP_K, tm), lambda i: (0, i))
    return pl.pallas_call(
        _router_kernel, grid=(T // tm,),
        in_specs=[pl.BlockSpec((tm, D), lambda i: (i, 0)), pl.BlockSpec((E, D), lambda i: (0, 0)),
                  pl.BlockSpec((E, 1), lambda i: (0, 0))],
        out_specs=[tok, tok, tok, pl.BlockSpec((E, 128), lambda i: (0, 0))],
        out_shape=[jax.ShapeDtypeStruct((TOP_K, T), jnp.int32), jax.ShapeDtypeStruct((TOP_K, T), F32),
                   jax.ShapeDtypeStruct((TOP_K, T), jnp.int32), jax.ShapeDtypeStruct((E, 128), jnp.int32)],
        scratch_shapes=[pltpu.VMEM((E, 128), F32)],
        compiler_params=_params("arbitrary"), name="router")(hb, w_router_t, b_router.reshape(E, 1))


def _moe_kernel(be_ref, bn_ref, src_hbm, h_hbm, wup_ref, bup_ref, wdn_ref, bdn_ref, y_ref,
                idx_ref, xbuf, idx_sem, row_sem, *, n_blocks):
    bm = xbuf.shape[1]
    ed = wdn_ref.shape[1]
    i = pl.program_id(0)
    slot = i % 2

    def idx_copy(blk, s):
        return pltpu.make_async_copy(src_hbm.at[blk], idx_ref.at[s], idx_sem.at[s])

    def gather_rows(s):
        def body(r, _):
            tok = idx_ref[s, r]
            pltpu.make_async_copy(h_hbm.at[pl.ds(tok, 1)], xbuf.at[s, pl.ds(r, 1)], row_sem.at[s]).start()
            return 0
        lax.fori_loop(0, bm, body, 0, unroll=8)

    @pl.when(i == 0)
    def _():
        idx_copy(0, 0).start()
        idx_copy(0, 0).wait()
        gather_rows(0)

        @pl.when(n_blocks > 1)
        def _():
            idx_copy(1, 1).start()

    @pl.when(i + 1 < n_blocks)
    def _():
        idx_copy(i + 1, 1 - slot).wait()
        gather_rows(1 - slot)

    @pl.when(i + 2 < n_blocks)
    def _():
        idx_copy(i + 2, slot).start()

    pltpu.make_async_copy(xbuf.at[slot], xbuf.at[slot], row_sem.at[slot]).wait()

    @pl.when(bn_ref[i] > 0)
    def _():
        x = xbuf[slot].astype(BF16)
        gu = jnp.dot(x, wup_ref[0], preferred_element_type=F32) + bup_ref[0]
        glu = jnp.minimum(gu[:, :ed], SWIGLU_LIMIT)
        lin = jnp.clip(gu[:, ed:], -SWIGLU_LIMIT, SWIGLU_LIMIT)
        act = glu * jax.nn.sigmoid(SWIGLU_ALPHA * glu) * (lin + 1.0)
        y_ref[...] = jnp.dot(act.astype(BF16), wdn_ref[0], preferred_element_type=F32) + bdn_ref[0]

    @pl.when(bn_ref[i] == 0)
    def _():
        y_ref[...] = jnp.zeros_like(y_ref)


def _moe(h, slot_src, blk_e, blk_n, w_up, b_up, w_down, b_down):
    T, D = h.shape
    n_blocks, bm = slot_src.shape
    E, _, two_ed = w_up.shape
    ed = two_ed // 2
    kern = functools.partial(_moe_kernel, n_blocks=n_blocks)
    grid_spec = pltpu.PrefetchScalarGridSpec(
        num_scalar_prefetch=2, grid=(n_blocks,),
        in_specs=[pl.BlockSpec(memory_space=pl.ANY), pl.BlockSpec(memory_space=pl.ANY),
                  pl.BlockSpec((1, D, two_ed), lambda i, be, bn: (be[i], 0, 0)),
                  pl.BlockSpec((1, 1, two_ed), lambda i, be, bn: (be[i], 0, 0)),
                  pl.BlockSpec((1, ed, D), lambda i, be, bn: (be[i], 0, 0)),
                  pl.BlockSpec((1, 1, D), lambda i, be, bn: (be[i], 0, 0))],
        out_specs=pl.BlockSpec((bm, D), lambda i, be, bn: (i, 0)),
        scratch_shapes=[pltpu.SMEM((2, bm), jnp.int32), pltpu.VMEM((2, bm, D), F32),
                        pltpu.SemaphoreType.DMA((2,)), pltpu.SemaphoreType.DMA((2,))])
    return pl.pallas_call(
        kern, grid_spec=grid_spec, out_shape=jax.ShapeDtypeStruct((n_blocks * bm, D), F32),
        compiler_params=_params("arbitrary"), name="moe_experts")(
            blk_e, blk_n, slot_src, h, w_up, b_up.reshape(E, 1, two_ed), w_down, b_down.reshape(E, 1, D))


def _combine_kernel(dst_hbm, y_hbm, gates_ref, h_ref, g_ref, b_ref, ho_ref, hbo_ref,
                    idx_ref, ybuf, idx_sem, row_sem, *, n_tiles, alpha):
    tm = h_ref.shape[0]
    i = pl.program_id(0)
    slot = i % 2

    def idx_copy(tile, s):
        return pltpu.make_async_copy(dst_hbm.at[tile], idx_ref.at[s], idx_sem.at[s])

    def gather_rows(s):
        for k in range(TOP_K):
            def body(r, _):
                row = idx_ref[s, k * tm + r]
                pltpu.make_async_copy(y_hbm.at[pl.ds(row, 1)], ybuf.at[s, k, pl.ds(r, 1)], row_sem.at[s]).start()
                return 0
            lax.fori_loop(0, tm, body, 0, unroll=8)

    @pl.when(i == 0)
    def _():
        idx_copy(0, 0).start()
        idx_copy(0, 0).wait()
        gather_rows(0)

        @pl.when(n_tiles > 1)
        def _():
            idx_copy(1, 1).start()

    @pl.when(i + 1 < n_tiles)
    def _():
        idx_copy(i + 1, 1 - slot).wait()
        gather_rows(1 - slot)

    @pl.when(i + 2 < n_tiles)
    def _():
        idx_copy(i + 2, slot).start()

    pltpu.make_async_copy(ybuf.at[slot], ybuf.at[slot], row_sem.at[slot]).wait()

    ffn = gates_ref[:, 0:1] * ybuf[slot, 0]
    for k in range(1, TOP_K):
        ffn = ffn + gates_ref[:, k:k + 1] * ybuf[slot, k]
    hn = _layer_norm(alpha * h_ref[...] + ffn, g_ref[...], b_ref[...])
    ho_ref[...] = hn
    hbo_ref[...] = hn.astype(BF16)


def _combine(dest_tiles, y_sorted, gates_t, h, g, b, alpha):
    T, D = h.shape
    n_tiles, ktm = dest_tiles.shape
    tm = ktm // TOP_K
    row = pl.BlockSpec((tm, D), lambda i: (i, 0))
    vec = pl.BlockSpec((1, D), lambda i: (0, 0))
    kern = functools.partial(_combine_kernel, n_tiles=n_tiles, alpha=alpha)
    return pl.pallas_call(
        kern, grid=(n_tiles,),
        in_specs=[pl.BlockSpec(memory_space=pl.ANY), pl.BlockSpec(memory_space=pl.ANY),
                  pl.BlockSpec((tm, TOP_K), lambda i: (i, 0)), row, vec, vec],
        out_specs=[row, row],
        out_shape=[jax.ShapeDtypeStruct((T, D), F32), jax.ShapeDtypeStruct((T, D), BF16)],
        scratch_shapes=[pltpu.SMEM((2, ktm), jnp.int32), pltpu.VMEM((2, TOP_K, tm, D), F32),
                        pltpu.SemaphoreType.DMA((2,)), pltpu.SemaphoreType.DMA((2,))],
        compiler_params=_params("arbitrary"), name="moe_combine_ln2")(
            dest_tiles, y_sorted, gates_t, h, g.reshape(1, D), b.reshape(1, D))


def _routing_plan(top_e, rank, counts, n_blocks, combine_tm):
    T = top_e.shape[1]
    bm = MOE_BLOCK
    padded = (counts + bm - 1) // bm * bm
    pad_end = jnp.cumsum(padded)
    pad_start = pad_end - padded
    dest = pad_start[top_e] + rank
    tok = jnp.broadcast_to(jnp.arange(T, dtype=jnp.int32)[None, :], dest.shape)
    slot_src = jnp.zeros((n_blocks * bm,), jnp.int32).at[dest.reshape(-1)].set(tok.reshape(-1))
    blk_start = jnp.arange(n_blocks, dtype=jnp.int32) * bm
    blk_e = jnp.minimum(jnp.searchsorted(pad_end, blk_start, side='right'), N_EXPERTS - 1).astype(jnp.int32)
    blk_n = jnp.clip(counts[blk_e] - (blk_start - pad_start[blk_e]), 0, bm).astype(jnp.int32)
    dest_tiles = dest.reshape(TOP_K, T // combine_tm, combine_tm).transpose(1, 0, 2).reshape(T // combine_tm, -1)
    return slot_src.reshape(n_blocks, bm), blk_e, blk_n, dest_tiles


def kernel(x, mem, ln_in_g, ln_in_b, w_in, pool_w, pool_scale, ret_log_gamma, ret_gn_g, sgu_ln_g, sgu_ln_b, sgu_w, sgu_b, w_mem_kv, w_branch, w_gate, b_gate, w_out, ln1_g, ln1_b, w_router, b_router, w_up, b_up, w_down, b_down, ln2_g, ln2_b):
    B, S, D = x.shape
    depth = w_in.shape[0]
    n_mem = mem.shape[1]
    T = B * S
    alpha = (2 * depth) ** 0.25
    head_dim = ret_gn_g.shape[1] // N_RET_HEADS
    combine_tm = 128
    n_blocks = -(-(T * TOP_K + N_EXPERTS * (MOE_BLOCK - 1)) // MOE_BLOCK)

    pos = jnp.arange(S, dtype=F32)
    inv_freq = jnp.exp(-math.log(ROPE_BASE) * jnp.arange(0, head_dim, 2, dtype=F32) / head_dim)
    ang = pos[:, None] * inv_freq[None, :]
    cos, sin = jnp.cos(ang), jnp.sin(ang)
    cos2 = jnp.concatenate([cos, cos], axis=1)
    sin2 = jnp.concatenate([-sin, sin], axis=1)

    mem_b = mem.reshape(B * n_mem, D).astype(BF16)
    h, hb = _ln_in(x.reshape(T, D), ln_in_g, ln_in_b)
    for l in range(depth):
        z = _matmul(hb, w_in[l].astype(BF16), tm=1024, tn=1024, name="in_proj")
        kv = _matmul(mem_b, w_mem_kv[l].astype(BF16), tm=B * n_mem, tn=1024, name="mem_kv")
        ys = (
            _pool(z, pool_w[l].astype(BF16), pool_scale[l], S),
            _retention(z, ret_log_gamma[l], ret_gn_g[l], cos2, sin2, B, S),
            _sgu(z, sgu_ln_g[l], sgu_ln_b[l], sgu_w[l].astype(BF16), sgu_b[l]),
            _mem_attn(z, kv, S, n_mem),
        )
        merged = _gate_merge(hb, ys, w_gate[l].astype(BF16), b_gate[l], w_branch[l].astype(BF16))
        h, hb = _out_ln(merged, w_out[l].astype(BF16), h, ln1_g[l], ln1_b[l], alpha)

        top_e, gates, rank, cnt = _router(hb, w_router[l].T.astype(BF16), b_router[l])
        slot_src, blk_e, blk_n, dest_tiles = _routing_plan(top_e, rank, cnt[:, 0], n_blocks, combine_tm)
        y_sorted = _moe(h, slot_src, blk_e, blk_n, w_up[l].astype(BF16), b_up[l], w_down[l].astype(BF16), b_down[l])
        h, hb = _combine(dest_tiles, y_sorted, gates.T, h, ln2_g[l], ln2_b[l], alpha)
    return h.reshape(B, S, D)
```

```python
import functools
import math

import jax
import jax.numpy as jnp
from jax import lax
from jax.experimental import pallas as pl
from jax.experimental.pallas import tpu as pltpu

F32 = jnp.float32
BF16 = jnp.bfloat16
U32 = jnp.uint32

N_BRANCHES = 4
POOL_WINDOWS = (2, 4, 8, 16)
N_RET_HEADS = 8
RET_CHUNK = 128
ROPE_BASE = 10000.0
SGU_CHUNK = 128
N_SGU_GROUPS = 4
N_MEM_HEADS = 4
N_EXPERTS = 32
TOP_K = 4
SWIGLU_LIMIT = 7.0
SWIGLU_ALPHA = 1.702
LN_EPS = 1e-5

V7X_VMEM_BYTES = 64 * 1024 * 1024
VMEM_LIMIT_BYTES = V7X_VMEM_BYTES - 8 * 1024 * 1024
BF16_SUBLANE_TILE = 16

MOE_BLOCK = 256
ROUTE_TILE = 128


def _params(*semantics):
    return pltpu.CompilerParams(dimension_semantics=semantics, vmem_limit_bytes=VMEM_LIMIT_BYTES)


def _rows(v):
    return v.reshape(v.shape[0], 1, v.shape[1])


def _layer_norm(x, g, b):
    mu = jnp.mean(x, axis=-1, keepdims=True)
    xc = x - mu
    var = jnp.mean(xc * xc, axis=-1, keepdims=True)
    return xc * lax.rsqrt(var + LN_EPS) * g + b


def _pack_bf16_pairs(x):
    n = x.shape[1] // 2
    xb = x.astype(BF16).astype(F32)
    lo = lax.bitcast_convert_type(xb[:, :n], U32)
    hi = lax.bitcast_convert_type(xb[:, n:], U32)
    return (lo >> 16) | hi


def _unpack_bf16_pairs(p):
    lo = lax.bitcast_convert_type(p << 16, F32).astype(BF16)
    hi = lax.bitcast_convert_type(p & jnp.uint32(0xFFFF0000), F32).astype(BF16)
    return jnp.concatenate([lo, hi], axis=1)


def _ln_in_kernel(x_ref, g_ref, b_ref, h_ref, hb_ref):
    h = _layer_norm(x_ref[...], g_ref[...], b_ref[...])
    h_ref[...] = h
    hb_ref[...] = h.astype(BF16)


def _ln_in(x, g, b, tm=512):
    T, D = x.shape
    row = pl.BlockSpec((tm, D), lambda i: (i, 0))
    vec = pl.BlockSpec((1, D), lambda i: (0, 0))
    return pl.pallas_call(
        _ln_in_kernel, grid=(T // tm,), in_specs=[row, vec, vec], out_specs=[row, row],
        out_shape=[jax.ShapeDtypeStruct((T, D), F32), jax.ShapeDtypeStruct((T, D), BF16)],
        compiler_params=_params("parallel"), name="ln_in")(x, g.reshape(1, D), b.reshape(1, D))


def _matmul_kernel(a_ref, w_ref, o_ref, wb_ref):
    @pl.when(pl.program_id(1) == 0)
    def _():
        wb_ref[...] = w_ref[0].astype(BF16)

    o_ref[...] = jnp.dot(a_ref[...], wb_ref[...], preferred_element_type=F32).astype(o_ref.dtype)


def _matmul(a, w, layer, tm, tn, name):
    M, K = a.shape
    N = w.shape[2]
    return pl.pallas_call(
        _matmul_kernel, grid=(N // tn, M // tm),
        in_specs=[pl.BlockSpec((tm, K), lambda j, i: (i, 0)), pl.BlockSpec((1, K, tn), lambda j, i: (layer, 0, j))],
        out_specs=pl.BlockSpec((tm, tn), lambda j, i: (i, j)),
        out_shape=jax.ShapeDtypeStruct((M, N), BF16),
        scratch_shapes=[pltpu.VMEM((K, tn), BF16)],
        compiler_params=_params("arbitrary", "arbitrary"), name=name)(a, w)


def _pool_kernel(z_ref, prev_ref, next_ref, w_ref, scale_ref, o_ref, ext_ref, *, seq, ts):
    halo = BF16_SUBLANE_TILE
    gd = w_ref.shape[2]
    pos0 = (pl.program_id(0) % (seq // ts)) * ts
    ext_ref[0:halo, :] = jnp.where(pos0 == 0, 0.0, prev_ref[...].astype(F32))
    ext_ref[halo:halo + ts, :] = z_ref[...].astype(F32)
    ext_ref[halo + ts:, :] = jnp.where(pos0 + ts == seq, 0.0, next_ref[...].astype(F32))
    pos = pos0 + lax.broadcasted_iota(jnp.int32, (ts, gd), 0)
    for g, w in enumerate(POOL_WINDOWS):
        cols = slice(g * gd, (g + 1) * gd)
        half = w // 2
        acc = ext_ref[halo - half:halo - half + ts, cols]
        for d in range(-half + 1, half):
            acc = acc + ext_ref[halo + d:halo + d + ts, cols]
        cnt = jnp.minimum(pos + half, seq) - jnp.maximum(pos - half, 0)
        mixed = acc / cnt.astype(F32) - ext_ref[halo:halo + ts, cols]
        y = jnp.dot(mixed.astype(BF16), w_ref[0, g], preferred_element_type=F32)
        o_ref[:, cols] = (y * scale_ref[0, :, cols]).astype(o_ref.dtype)


def _pool(z, pool_w, pool_scale, layer, seq, ts=512):
    T = z.shape[0]
    _, G, gd, _ = pool_w.shape
    width = G * gd
    halo = BF16_SUBLANE_TILE
    n_halo = T // halo
    kern = functools.partial(_pool_kernel, seq=seq, ts=ts)
    return pl.pallas_call(
        kern, grid=(T // ts,),
        in_specs=[
            pl.BlockSpec((ts, width), lambda i: (i, 0)),
            pl.BlockSpec((halo, width), lambda i: (jnp.maximum(i * (ts // halo) - 1, 0), 0)),
            pl.BlockSpec((halo, width), lambda i: (jnp.minimum((i + 1) * (ts // halo), n_halo - 1), 0)),
            pl.BlockSpec((1, G, gd, gd), lambda i: (layer, 0, 0, 0)),
            pl.BlockSpec((1, 1, width), lambda i: (layer, 0, 0)),
        ],
        out_specs=pl.BlockSpec((ts, width), lambda i: (i, 0)),
        out_shape=jax.ShapeDtypeStruct((T, width), BF16),
        scratch_shapes=[pltpu.VMEM((ts + 2 * halo, width), F32)],
        compiler_params=_params("parallel"), name="pool")(z, z, z, pool_w, _rows(pool_scale))


def _retention_kernel(lg_ref, q_ref, k_ref, v_ref, g_ref, cos_ref, sin_ref, gn_ref, o_ref,
                      rs_ref, kr_ref, kvf_ref, *, seq, unroll):
    C = RET_CHUNK
    d = q_ref.shape[1]
    nc = seq // C
    head = pl.program_id(0) % N_RET_HEADS
    lgf = lg_ref[0, head]
    lgb = lg_ref[1, head]
    row = lax.broadcasted_iota(jnp.int32, (C, C), 0).astype(F32)
    col = lax.broadcasted_iota(jnp.int32, (C, C), 1).astype(F32)
    diff = row - col
    decay = jnp.where(diff >= 0, jnp.exp(lgf * jnp.maximum(diff, 0.0)), jnp.exp(lgb * jnp.maximum(-diff, 0.0)))
    rowd = lax.broadcasted_iota(jnp.int32, (C, d), 0).astype(F32)
    qwf = jnp.exp(lgf * (rowd + 1.0))
    qwb = jnp.exp(lgb * (C - rowd))
    kwf = jnp.exp(lgf * (C - 1.0 - rowd))
    kwb = jnp.exp(lgb * rowd)
    chunk_decay_f = jnp.exp(lgf * jnp.full((d, d), float(C), F32))
    chunk_decay_b = jnp.exp(lgb * jnp.full((d, d), float(C), F32))
    tn_dims = (((0,), (0,)), ((), ()))
    nt_dims = (((1,), (1,)), ((), ()))

    def rotary(ref, sl):
        x = ref[sl, :].astype(F32)
        return x * cos_ref[sl, :] + pltpu.roll(x, d // 2, axis=1) * sin_ref[sl, :]

    def chunk(n):
        return pl.ds(pl.multiple_of(n * C, C), C)

    def bwd_body(j, state):
        n = nc - 1 - j
        sl = chunk(n)
        rs_ref[n] = state.astype(BF16)
        k = rotary(k_ref, sl) * (d ** -0.5)
        kr_ref[sl, :] = k.astype(BF16)
        kw = jnp.concatenate([k * kwf, k * kwb], axis=1).astype(BF16)
        kv = lax.dot_general(kw, v_ref[sl, :], tn_dims, preferred_element_type=F32)
        kvf_ref[n] = kv[:d]
        return state * chunk_decay_b + kv[d:]

    lax.fori_loop(0, nc, bwd_body, jnp.zeros((d, d), F32), unroll=unroll)

    def fwd_body(n, state):
        sl = chunk(n)
        qb = rotary(q_ref, sl).astype(BF16)
        v = v_ref[sl, :]
        scores = lax.dot_general(qb, kr_ref[sl, :], nt_dims, preferred_element_type=F32) * decay
        y = jnp.dot(scores.astype(BF16), v, preferred_element_type=F32)
        states = jnp.concatenate([state.astype(BF16), rs_ref[n]], axis=1)
        cross = jnp.dot(qb, states, preferred_element_type=F32)
        y = y + cross[:, :d] * qwf + cross[:, d:] * qwb
        mu = jnp.mean(y, axis=-1, keepdims=True)
        yc = y - mu
        var = jnp.mean(yc * yc, axis=-1, keepdims=True)
        yn = yc * lax.rsqrt(var + LN_EPS) * gn_ref[0]
        gate = g_ref[sl, :].astype(F32)
        o_ref[sl, :] = (yn * (gate * jax.nn.sigmoid(gate))).astype(o_ref.dtype)
        return state * chunk_decay_f + kvf_ref[n]

    lax.fori_loop(0, nc, fwd_body, jnp.zeros((d, d), F32), unroll=unroll)


def _retention(z, log_gamma, gn_g, cos2, sin2, layer, batch, seq, unroll=2):
    T = z.shape[0]
    H = N_RET_HEADS
    d = gn_g.shape[1] // H
    nc = seq // RET_CHUNK

    def col(split):
        return pl.BlockSpec((seq, d), lambda i: (i // H, split * H + i % H))

    table = pl.BlockSpec((seq, d), lambda i: (0, 0))
    kern = functools.partial(_retention_kernel, seq=seq, unroll=unroll)
    return pl.pallas_call(
        kern, grid=(batch * H,),
        in_specs=[pl.BlockSpec(memory_space=pltpu.SMEM), col(1), col(2), col(3), col(4), table, table,
                  pl.BlockSpec((1, 1, d), lambda i: (layer, 0, i % H))],
        out_specs=pl.BlockSpec((seq, d), lambda i: (i // H, i % H)),
        out_shape=jax.ShapeDtypeStruct((T, H * d), BF16),
        scratch_shapes=[pltpu.VMEM((nc, d, d), BF16), pltpu.VMEM((seq, d), BF16), pltpu.VMEM((nc, d, d), F32)],
        compiler_params=_params("parallel"), name="retention")(
            log_gamma[layer], z, z, z, z, cos2, sin2, _rows(gn_g))


def _sgu_kernel(u_ref, v_ref, lng_ref, lnb_ref, w_ref, bias_ref, o_ref, *, chunks):
    P = SGU_CHUNK
    gd = o_ref.shape[1] // N_SGU_GROUPS

    def body(c, _):
        rows = pl.ds(pl.multiple_of(c * P, P), P)
        vn = _layer_norm(jax.nn.gelu(v_ref[rows, :].astype(F32)), lng_ref[0], lnb_ref[0]).astype(BF16)
        for g in range(N_SGU_GROUPS):
            cols = slice(g * gd, (g + 1) * gd)
            mixed = jnp.dot(w_ref[0, g], vn[:, cols], preferred_element_type=F32) + bias_ref[:, cols]
            u = jax.nn.gelu(u_ref[rows, cols].astype(F32))
            o_ref[rows, cols] = (u * mixed).astype(o_ref.dtype)
        return 0

    lax.fori_loop(0, chunks, body, 0)


def _sgu(z, ln_g, ln_b, w_s, b_s, layer, chunks=4):
    T = z.shape[0]
    width = ln_g.shape[1]
    _, G, P, _ = w_s.shape
    rows = chunks * P
    bias = jnp.repeat(b_s[layer].T.astype(F32), width // G, axis=1)
    kern = functools.partial(_sgu_kernel, chunks=chunks)
    vec = pl.BlockSpec((1, 1, width), lambda i: (layer, 0, 0))
    return pl.pallas_call(
        kern, grid=(T // rows,),
        in_specs=[pl.BlockSpec((rows, width), lambda i: (i, 5)), pl.BlockSpec((rows, width), lambda i: (i, 6)),
                  vec, vec, pl.BlockSpec((1, G, P, P), lambda i: (layer, 0, 0, 0)),
                  pl.BlockSpec((P, width), lambda i: (0, 0))],
        out_specs=pl.BlockSpec((rows, width), lambda i: (i, 0)),
        out_shape=jax.ShapeDtypeStruct((T, width), BF16),
        compiler_params=_params("parallel"), name="sgu")(z, z, _rows(ln_g), _rows(ln_b), w_s, bias)


def _mem_attn_kernel(q_ref, kv_ref, o_ref):
    width = o_ref.shape[1]
    hd = width // N_MEM_HEADS
    nt_dims = (((1,), (1,)), ((), ()))
    for h in range(N_MEM_HEADS):
        cols = slice(h * hd, (h + 1) * hd)
        s = lax.dot_general(q_ref[:, cols], kv_ref[:, cols], nt_dims, preferred_element_type=F32) * (hd ** -0.5)
        p = jnp.exp(s - jnp.max(s, axis=-1, keepdims=True))
        p = p / jnp.sum(p, axis=-1, keepdims=True)
        v = kv_ref[:, width + h * hd:width + (h + 1) * hd]
        o_ref[:, cols] = jnp.dot(p.astype(BF16), v, preferred_element_type=F32).astype(o_ref.dtype)


def _mem_attn(z, kv, seq, n_mem, tm=512):
    T = z.shape[0]
    width = kv.shape[1] // 2
    return pl.pallas_call(
        _mem_attn_kernel, grid=(T // tm,),
        in_specs=[pl.BlockSpec((tm, width), lambda i: (i, 7)),
                  pl.BlockSpec((n_mem, 2 * width), lambda i: (i // (seq // tm), 0))],
        out_specs=pl.BlockSpec((tm, width), lambda i: (i, 0)),
        out_shape=jax.ShapeDtypeStruct((T, width), BF16),
        compiler_params=_params("parallel"), name="mem_attn")(z, kv)


def _gate_merge_kernel(hb_ref, y0_ref, y1_ref, y2_ref, y3_ref, wg_ref, bg_ref, wb_ref, o_ref):
    acc = None
    for b, y_ref in enumerate((y0_ref, y1_ref, y2_ref, y3_ref)):
        gate = jax.nn.sigmoid(jnp.dot(hb_ref[...], wg_ref[0, b], preferred_element_type=F32) + bg_ref[0, b:b + 1, :])
        term = gate * jnp.dot(y_ref[...], wb_ref[0, b], preferred_element_type=F32)
        acc = term if acc is None else acc + term
    o_ref[...] = acc.astype(o_ref.dtype)


def _gate_merge(hb, ys, w_gate, b_gate, w_branch, layer, tm=512, tn=512):
    T, D = hb.shape
    bw = w_branch.shape[2]
    yspec = pl.BlockSpec((tm, bw), lambda j, i: (i, 0))
    return pl.pallas_call(
        _gate_merge_kernel, grid=(D // tn, T // tm),
        in_specs=[pl.BlockSpec((tm, D), lambda j, i: (i, 0)), yspec, yspec, yspec, yspec,
                  pl.BlockSpec((1, N_BRANCHES, D, tn), lambda j, i: (layer, 0, 0, j)),
                  pl.BlockSpec((1, N_BRANCHES, tn), lambda j, i: (layer, 0, j)),
                  pl.BlockSpec((1, N_BRANCHES, bw, tn), lambda j, i: (layer, 0, 0, j))],
        out_specs=pl.BlockSpec((tm, tn), lambda j, i: (i, j)),
        out_shape=jax.ShapeDtypeStruct((T, D), BF16),
        compiler_params=_params("parallel", "parallel"), name="gate_merge")(hb, *ys, w_gate, b_gate, w_branch)


def _out_ln_kernel(m_ref, w_ref, h_ref, g_ref, b_ref, ho_ref, hbo_ref, hpo_ref, *, alpha):
    mix = jnp.dot(m_ref[...], w_ref[0], preferred_element_type=F32)
    hn = _layer_norm(alpha * h_ref[...] + mix, g_ref[0], b_ref[0])
    ho_ref[...] = hn
    hbo_ref[...] = hn.astype(BF16)
    hpo_ref[...] = _pack_bf16_pairs(hn)


def _out_ln(merged, w_out, h, g, b, layer, alpha, tm=256):
    T, D = h.shape
    row = pl.BlockSpec((tm, D), lambda i: (i, 0))
    vec = pl.BlockSpec((1, 1, D), lambda i: (layer, 0, 0))
    kern = functools.partial(_out_ln_kernel, alpha=alpha)
    return pl.pallas_call(
        kern, grid=(T // tm,),
        in_specs=[row, pl.BlockSpec((1, D, D), lambda i: (layer, 0, 0)), row, vec, vec],
        out_specs=[row, row, pl.BlockSpec((tm, D // 2), lambda i: (i, 0))],
        out_shape=[jax.ShapeDtypeStruct((T, D), F32), jax.ShapeDtypeStruct((T, D), BF16),
                   jax.ShapeDtypeStruct((T, D // 2), U32)],
        compiler_params=_params("parallel"), name="out_ln1")(merged, w_out, h, _rows(g), _rows(b))


def _router_kernel(hb_ref, wr_ref, br_ref, e_ref, gate_ref, rank_ref, cnt_ref, carry_ref):
    tm = hb_ref.shape[0]

    @pl.when(pl.program_id(0) == 0)
    def _():
        carry_ref[...] = jnp.zeros_like(carry_ref)

    nt_dims = (((1,), (1,)), ((), ()))
    logits = lax.dot_general(wr_ref[...], hb_ref[...], nt_dims, preferred_element_type=F32) + br_ref[...]
    eidx = lax.broadcasted_iota(jnp.int32, logits.shape, 0)
    work = logits
    tops, hots = [], []
    for k in range(TOP_K):
        m = jnp.max(work, axis=0, keepdims=True)
        sel = jnp.min(jnp.where(work == m, eidx, N_EXPERTS), axis=0, keepdims=True)
        hot = eidx == sel
        work = jnp.where(hot, -jnp.inf, work)
        e_ref[k:k + 1, :] = sel
        tops.append(m)
        hots.append(hot)
    ex = [jnp.exp(t - tops[0]) for t in tops]
    den = ex[0] + ex[1] + ex[2] + ex[3]
    member = jnp.zeros(logits.shape, F32)
    for hot in hots:
        member = member + jnp.where(hot, 1.0, 0.0)
    tri = jnp.where(lax.broadcasted_iota(jnp.int32, (tm, tm), 0) <= lax.broadcasted_iota(jnp.int32, (tm, tm), 1),
                    1.0, 0.0).astype(BF16)
    incl = jnp.dot(member.astype(BF16), tri, preferred_element_type=F32)
    before = carry_ref[:, 0:1] + incl - member
    for k in range(TOP_K):
        gate_ref[k:k + 1, :] = ex[k] / den
        rank_ref[k:k + 1, :] = jnp.sum(jnp.where(hots[k], before, 0.0), axis=0, keepdims=True).astype(jnp.int32)
    carry_ref[...] = carry_ref[...] + incl[:, tm - 1:tm]
    cnt_ref[...] = carry_ref[...].astype(jnp.int32)


def _router(hb, w_router_t, b_router, tm=512):
    T, D = hb.shape
    E = w_router_t.shape[0]
    tok = pl.BlockSpec((TOP_K, tm), lambda i: (0, i))
    return pl.pallas_call(
        _router_kernel, grid=(T // tm,),
        in_specs=[pl.BlockSpec((tm, D), lambda i: (i, 0)), pl.BlockSpec((E, D), lambda i: (0, 0)),
                  pl.BlockSpec((E, 1), lambda i: (0, 0))],
        out_specs=[tok, tok, tok, pl.BlockSpec((E, 128), lambda i: (0, 0))],
        out_shape=[jax.ShapeDtypeStruct((TOP_K, T), jnp.int32), jax.ShapeDtypeStruct((TOP_K, T), F32),
                   jax.ShapeDtypeStruct((TOP_K, T), jnp.int32), jax.ShapeDtypeStruct((E, 128), jnp.int32)],
        scratch_shapes=[pltpu.VMEM((E, 128), F32)],
        compiler_params=_params("arbitrary"), name="router")(hb, w_router_t, b_router.reshape(E, 1))


def _dispatch_kernel(plo_ref, phi_ref, dst_hbm, hp_ref, xs_hbm, idx_ref, zero_ref, idx_sem, row_sem, zero_sem,
                     *, n_tiles, n_blocks):
    tm = hp_ref.shape[0]
    bm = zero_ref.shape[0]
    i = pl.program_id(0)
    slot = i % 2

    def idx_copy(tile, s):
        return pltpu.make_async_copy(dst_hbm.at[tile], idx_ref.at[s], idx_sem.at[s])

    def zero_row(r):
        return pltpu.make_async_copy(zero_ref.at[pl.ds(0, 1)], xs_hbm.at[pl.ds(r, 1)], zero_sem)

    def zero_block(blk):
        return pltpu.make_async_copy(zero_ref, xs_hbm.at[pl.ds(pl.multiple_of(blk * bm, bm), bm)], zero_sem)

    def for_each(lo, hi, fn):
        def body(r, _):
            fn(r)
            return 0
        lax.fori_loop(lo, hi, body, 0)

    @pl.when(i == 0)
    def _():
        idx_copy(0, 0).start()
        zero_ref[...] = jnp.zeros_like(zero_ref)
        for e in range(N_EXPERTS):
            for_each(plo_ref[e], phi_ref[e], lambda r: zero_row(r).start())
            for_each(plo_ref[e], phi_ref[e], lambda r: zero_row(r).wait())
        first_unused = phi_ref[N_EXPERTS - 1] // bm
        for_each(first_unused, n_blocks, lambda blk: zero_block(blk).start())
        for_each(first_unused, n_blocks, lambda blk: zero_block(blk).wait())

    idx_copy(i, slot).wait()

    @pl.when(i + 1 < n_tiles)
    def _():
        idx_copy(i + 1, 1 - slot).start()

    for k in range(TOP_K):
        for r in range(tm):
            d = idx_ref[slot, k * tm + r]
            pltpu.make_async_copy(hp_ref.at[pl.ds(r, 1)], xs_hbm.at[pl.ds(d, 1)], row_sem).start(priority=r % 2)
    for k in range(TOP_K):
        pltpu.make_async_copy(hp_ref, xs_hbm.at[pl.ds(0, tm)], row_sem).wait()


def _dispatch(hp, dest_tiles, pad_lo, pad_hi, n_slots):
    T, half = hp.shape
    n_tiles, ktm = dest_tiles.shape
    tm = ktm // TOP_K
    kern = functools.partial(_dispatch_kernel, n_tiles=n_tiles, n_blocks=n_slots // MOE_BLOCK)
    grid_spec = pltpu.PrefetchScalarGridSpec(
        num_scalar_prefetch=2, grid=(n_tiles,),
        in_specs=[pl.BlockSpec(memory_space=pl.ANY), pl.BlockSpec((tm, half), lambda i, lo, hi: (i, 0))],
        out_specs=pl.BlockSpec(memory_space=pl.ANY),
        scratch_shapes=[pltpu.SMEM((2, ktm), jnp.int32), pltpu.VMEM((MOE_BLOCK, half), U32),
                        pltpu.SemaphoreType.DMA((2,)), pltpu.SemaphoreType.DMA, pltpu.SemaphoreType.DMA])
    return pl.pallas_call(
        kern, grid_spec=grid_spec, out_shape=jax.ShapeDtypeStruct((n_slots, half), U32),
        compiler_params=_params("arbitrary"), name="moe_dispatch")(pad_lo, pad_hi, dest_tiles, hp)


def _moe_kernel(be_ref, bn_ref, x_ref, wup_ref, bup_ref, wdn_ref, bdn_ref, y_ref):
    ed = wdn_ref.shape[1]
    i = pl.program_id(0)

    @pl.when(bn_ref[i] > 0)
    def _():
        x = _unpack_bf16_pairs(x_ref[...])
        gu = jnp.dot(x, wup_ref[0], preferred_element_type=F32) + bup_ref[0]
        glu = jnp.minimum(gu[:, :ed], SWIGLU_LIMIT)
        lin = jnp.clip(gu[:, ed:], -SWIGLU_LIMIT, SWIGLU_LIMIT)
        act = glu * jax.nn.sigmoid(SWIGLU_ALPHA * glu) * (lin + 1.0)
        y_ref[...] = jnp.dot(act.astype(BF16), wdn_ref[0], preferred_element_type=F32) + bdn_ref[0]

    @pl.when(bn_ref[i] == 0)
    def _():
        y_ref[...] = jnp.zeros_like(y_ref)


def _moe(xs, blk_e, blk_n, w_up, b_up, w_down, b_down, layer):
    n_slots, half = xs.shape
    bm = MOE_BLOCK
    n_blocks = n_slots // bm
    _, D, two_ed = w_up.shape
    ed = two_ed // 2
    base = layer * N_EXPERTS

    def expert(i, be, bn):
        return (base + be[i], 0, 0)

    grid_spec = pltpu.PrefetchScalarGridSpec(
        num_scalar_prefetch=2, grid=(n_blocks,),
        in_specs=[pl.BlockSpec((bm, half), lambda i, be, bn: (jnp.where(bn[i] > 0, i, 0), 0)),
                  pl.BlockSpec((1, D, two_ed), expert), pl.BlockSpec((1, 1, two_ed), expert),
                  pl.BlockSpec((1, ed, D), expert), pl.BlockSpec((1, 1, D), expert)],
        out_specs=pl.BlockSpec((bm, D), lambda i, be, bn: (i, 0)))
    return pl.pallas_call(
        _moe_kernel, grid_spec=grid_spec, out_shape=jax.ShapeDtypeStruct((n_slots, D), F32),
        compiler_params=_params("arbitrary"), name="moe_experts")(blk_e, blk_n, xs, w_up, b_up, w_down, b_down)


def _combine_kernel(dst_hbm, y_hbm, gates_ref, h_ref, g_ref, b_ref, ho_ref, hbo_ref,
                    idx_ref, ybuf, idx_sem, row_sem, *, n_tiles, alpha):
    tm = h_ref.shape[0]
    i = pl.program_id(0)
    slot = i % 2

    def idx_copy(tile, s):
        return pltpu.make_async_copy(dst_hbm.at[tile], idx_ref.at[s], idx_sem.at[s])

    @pl.when(i == 0)
    def _():
        idx_copy(0, 0).start()

    @pl.when(i < n_tiles)
    def _():
        idx_copy(i, slot).wait()
        for k in range(TOP_K):
            for r in range(tm):
                row = idx_ref[slot, k * tm + r]
                pltpu.make_async_copy(y_hbm.at[pl.ds(row, 1)], ybuf.at[slot, k, pl.ds(r, 1)],
                                      row_sem.at[slot]).start(priority=r % 2)

    @pl.when(i + 1 < n_tiles)
    def _():
        idx_copy(i + 1, 1 - slot).start()

    @pl.when(i > 0)
    def _():
        prev = 1 - slot
        pltpu.make_async_copy(ybuf.at[prev], ybuf.at[prev], row_sem.at[prev]).wait()
        ffn = gates_ref[:, 0:1] * ybuf[prev, 0]
        for k in range(1, TOP_K):
            ffn = ffn + gates_ref[:, k:k + 1] * ybuf[prev, k]
        hn = _layer_norm(alpha * h_ref[...] + ffn, g_ref[0], b_ref[0])
        ho_ref[...] = hn
        hbo_ref[...] = hn.astype(BF16)


def _combine(dest_tiles, y_sorted, gates_t, h, g, b, layer, alpha):
    T, D = h.shape
    n_tiles, ktm = dest_tiles.shape
    tm = ktm // TOP_K

    def tile(i):
        return (jnp.maximum(i - 1, 0), 0)

    row = pl.BlockSpec((tm, D), tile)
    vec = pl.BlockSpec((1, 1, D), lambda i: (layer, 0, 0))
    kern = functools.partial(_combine_kernel, n_tiles=n_tiles, alpha=alpha)
    return pl.pallas_call(
        kern, grid=(n_tiles + 1,),
        in_specs=[pl.BlockSpec(memory_space=pl.ANY), pl.BlockSpec(memory_space=pl.ANY),
                  pl.BlockSpec((tm, TOP_K), tile), row, vec, vec],
        out_specs=[row, row],
        out_shape=[jax.ShapeDtypeStruct((T, D), F32), jax.ShapeDtypeStruct((T, D), BF16)],
        scratch_shapes=[pltpu.SMEM((2, ktm), jnp.int32), pltpu.VMEM((2, TOP_K, tm, D), F32),
                        pltpu.SemaphoreType.DMA((2,)), pltpu.SemaphoreType.DMA((2,))],
        compiler_params=_params("arbitrary"), name="moe_combine_ln2")(
            dest_tiles, y_sorted, gates_t, h, _rows(g), _rows(b))


def _routing_plan(top_e, rank, counts, n_blocks):
    T = top_e.shape[1]
    bm = MOE_BLOCK
    experts = jnp.arange(N_EXPERTS, dtype=jnp.int32)
    padded = (counts + bm - 1) // bm * bm
    pad_end = jnp.cumsum(padded)
    pad_start = pad_end - padded
    start_of = jnp.sum(jnp.where(top_e[None] == experts[:, None, None], pad_start[:, None, None], 0), axis=0)
    dest = start_of + rank
    blk_start = jnp.arange(n_blocks, dtype=jnp.int32) * bm
    blk_e = jnp.minimum(jnp.sum(pad_end[None, :] <= blk_start[:, None], axis=1), N_EXPERTS - 1).astype(jnp.int32)
    hot = blk_e[:, None] == experts[None, :]
    blk_cnt = jnp.sum(jnp.where(hot, counts[None, :], 0), axis=1)
    blk_pad_start = jnp.sum(jnp.where(hot, pad_start[None, :], 0), axis=1)
    blk_n = jnp.clip(blk_cnt - (blk_start - blk_pad_start), 0, bm).astype(jnp.int32)
    tm = ROUTE_TILE
    dest_tiles = dest.reshape(TOP_K, T // tm, tm).transpose(1, 0, 2).reshape(T // tm, TOP_K * tm)
    return dest_tiles, blk_e, blk_n, (pad_start + counts).astype(jnp.int32), pad_end.astype(jnp.int32)


def kernel(x, mem, ln_in_g, ln_in_b, w_in, pool_w, pool_scale, ret_log_gamma, ret_gn_g, sgu_ln_g, sgu_ln_b, sgu_w, sgu_b, w_mem_kv, w_branch, w_gate, b_gate, w_out, ln1_g, ln1_b, w_router, b_router, w_up, b_up, w_down, b_down, ln2_g, ln2_b):
    B, S, D = x.shape
    depth = w_in.shape[0]
    n_mem = mem.shape[1]
    T = B * S
    alpha = (2 * depth) ** 0.25
    head_dim = ret_gn_g.shape[1] // N_RET_HEADS
    n_blocks = -(-(T * TOP_K + N_EXPERTS * (MOE_BLOCK - 1)) // MOE_BLOCK)

    pos = jnp.arange(S, dtype=F32)
    inv_freq = jnp.exp(-math.log(ROPE_BASE) * jnp.arange(0, head_dim, 2, dtype=F32) / head_dim)
    ang = pos[:, None] * inv_freq[None, :]
    cos, sin = jnp.cos(ang), jnp.sin(ang)
    cos2 = jnp.concatenate([cos, cos], axis=1)
    sin2 = jnp.concatenate([-sin, sin], axis=1)

    pool_wb, sgu_wb = pool_w.astype(BF16), sgu_w.astype(BF16)
    w_gate_b, w_branch_b, w_out_b = w_gate.astype(BF16), w_branch.astype(BF16), w_out.astype(BF16)
    w_router_t = jnp.swapaxes(w_router, 1, 2).astype(BF16)
    E = w_up.shape[1]
    w_up_b = w_up.astype(BF16).reshape(depth * E, D, -1)
    w_down_b = w_down.astype(BF16).reshape(depth * E, -1, D)
    b_up_r = b_up.reshape(depth * E, 1, -1)
    b_down_r = b_down.reshape(depth * E, 1, D)

    mem_b = mem.reshape(B * n_mem, D).astype(BF16)
    h, hb = _ln_in(x.reshape(T, D), ln_in_g, ln_in_b)
    for l in range(depth):
        z = _matmul(hb, w_in, l, tm=1024, tn=1024, name="in_proj")
        kv = _matmul(mem_b, w_mem_kv, l, tm=B * n_mem, tn=1024, name="mem_kv")
        ys = (
            _pool(z, pool_wb, pool_scale, l, S),
            _retention(z, ret_log_gamma, ret_gn_g, cos2, sin2, l, B, S),
            _sgu(z, sgu_ln_g, sgu_ln_b, sgu_wb, sgu_b, l),
            _mem_attn(z, kv, S, n_mem),
        )
        merged = _gate_merge(hb, ys, w_gate_b, b_gate, w_branch_b, l)
        h, hb, hp = _out_ln(merged, w_out_b, h, ln1_g, ln1_b, l, alpha)

        top_e, gates, rank, cnt = _router(hb, w_router_t[l], b_router[l])
        dest_tiles, blk_e, blk_n, pad_lo, pad_hi = _routing_plan(top_e, rank, cnt[:, 0], n_blocks)
        xs = _dispatch(hp, dest_tiles, pad_lo, pad_hi, n_blocks * MOE_BLOCK)
        y_sorted = _moe(xs, blk_e, blk_n, w_up_b, b_up_r, w_down_b, b_down_r, l)
        h, hb = _combine(dest_tiles, y_sorted, gates.T, h, ln2_g, ln2_b, l, alpha)
    return h.reshape(B, S, D)
```

```python
import functools
import math

import jax
import jax.numpy as jnp
from jax import lax
from jax.experimental import pallas as pl
from jax.experimental.pallas import tpu as pltpu

F32 = jnp.float32
BF16 = jnp.bfloat16
U32 = jnp.uint32

N_BRANCHES = 4
POOL_WINDOWS = (2, 4, 8, 16)
N_RET_HEADS = 8
RET_CHUNK = 128
ROPE_BASE = 10000.0
SGU_CHUNK = 128
N_SGU_GROUPS = 4
N_MEM_HEADS = 4
N_EXPERTS = 32
TOP_K = 4
SWIGLU_LIMIT = 7.0
SWIGLU_ALPHA = 1.702
LN_EPS = 1e-5

V7X_VMEM_BYTES = 64 * 1024 * 1024
VMEM_LIMIT_BYTES = V7X_VMEM_BYTES - 8 * 1024 * 1024
BF16_SUBLANE_TILE = 16
LANES = 128

MOE_BLOCK = 256
ROUTE_TILE = 128
ISSUE_BATCH_TOKENS = 4


def _params(*semantics):
    return pltpu.CompilerParams(dimension_semantics=semantics, vmem_limit_bytes=VMEM_LIMIT_BYTES)


def _rows(v):
    return v.reshape(v.shape[0], 1, v.shape[1])


def _layer_norm(x, g, b):
    mu = jnp.mean(x, axis=-1, keepdims=True)
    xc = x - mu
    var = jnp.mean(xc * xc, axis=-1, keepdims=True)
    return xc * lax.rsqrt(var + LN_EPS) * g + b


def _pack_bf16_pairs(x):
    n = x.shape[1] // 2
    xb = x.astype(BF16).astype(F32)
    lo = lax.bitcast_convert_type(xb[:, :n], U32)
    hi = lax.bitcast_convert_type(xb[:, n:], U32)
    return (lo >> 16) | hi


def _store_row_major(ref, x, rows):
    n = x.shape[1] // LANES
    for j in range(n):
        ref[pl.ds(j, rows, stride=n), :] = x[:, j * LANES:(j + 1) * LANES]


def _load_row_major(ref, rows, n, index=()):
    return [ref[(*index, pl.ds(j, rows, stride=n), slice(None))] for j in range(n)]


def _unpack_bf16_pairs(p):
    lo = lax.bitcast_convert_type(p << 16, F32).astype(BF16)
    hi = lax.bitcast_convert_type(p & jnp.uint32(0xFFFF0000), F32).astype(BF16)
    return jnp.concatenate([lo, hi], axis=1)


def _ln_in_kernel(x_ref, g_ref, b_ref, h_ref, hb_ref):
    h = _layer_norm(x_ref[...], g_ref[...], b_ref[...])
    h_ref[...] = h
    hb_ref[...] = h.astype(BF16)


def _ln_in(x, g, b, tm=512):
    T, D = x.shape
    row = pl.BlockSpec((tm, D), lambda i: (i, 0))
    vec = pl.BlockSpec((1, D), lambda i: (0, 0))
    return pl.pallas_call(
        _ln_in_kernel, grid=(T // tm,), in_specs=[row, vec, vec], out_specs=[row, row],
        out_shape=[jax.ShapeDtypeStruct((T, D), F32), jax.ShapeDtypeStruct((T, D), BF16)],
        compiler_params=_params("parallel"), name="ln_in")(x, g.reshape(1, D), b.reshape(1, D))


class _SideCasts:
    def __init__(self, weights, layer, n_steps, step_of):
        self.arrays, self.in_specs, self.out_specs, self.out_shapes, self.shapes = [], [], [], [], []
        for w in weights:
            cols = w.shape[-1]
            rows = math.prod(w.shape[1:-1])
            slab = rows // n_steps
            assert slab * n_steps == rows and slab % BF16_SUBLANE_TILE == 0, (w.shape, n_steps)
            self.arrays.append(w.reshape(w.shape[0] * rows, cols))
            self.in_specs.append(pl.BlockSpec((slab, cols), lambda *g: (layer * n_steps + step_of(*g), 0)))
            self.out_specs.append(pl.BlockSpec((slab, cols), lambda *g: (step_of(*g), 0)))
            self.out_shapes.append(jax.ShapeDtypeStruct((rows, cols), BF16))
            self.shapes.append(w.shape[1:])

    def __len__(self):
        return len(self.arrays)

    @staticmethod
    def run(in_refs, out_refs):
        for i_ref, o_ref in zip(in_refs, out_refs):
            o_ref[...] = i_ref[...].astype(BF16)

    def unflatten(self, outs):
        return [o.reshape(s) for o, s in zip(outs, self.shapes)]


def _matmul_kernel(a_ref, w_ref, *refs, n_side):
    side_in, o_ref, side_out, wb_ref = refs[:n_side], refs[n_side], refs[n_side + 1:-1], refs[-1]

    @pl.when(pl.program_id(1) == 0)
    def _():
        wb_ref[...] = w_ref[0].astype(BF16)

    o_ref[...] = jnp.dot(a_ref[...], wb_ref[...], preferred_element_type=F32).astype(o_ref.dtype)
    _SideCasts.run(side_in, side_out)


def _matmul(a, w, layer, tm, tn, name, side_weights=()):
    M, K = a.shape
    N = w.shape[2]
    n_i = M // tm
    side = _SideCasts(side_weights, layer, (N // tn) * n_i, lambda j, i: j * n_i + i)
    outs = pl.pallas_call(
        functools.partial(_matmul_kernel, n_side=len(side)), grid=(N // tn, n_i),
        in_specs=[pl.BlockSpec((tm, K), lambda j, i: (i, 0)), pl.BlockSpec((1, K, tn), lambda j, i: (layer, 0, j)),
                  *side.in_specs],
        out_specs=[pl.BlockSpec((tm, tn), lambda j, i: (i, j)), *side.out_specs],
        out_shape=[jax.ShapeDtypeStruct((M, N), BF16), *side.out_shapes],
        scratch_shapes=[pltpu.VMEM((K, tn), BF16)],
        compiler_params=_params("arbitrary", "arbitrary"), name=name)(a, w, *side.arrays)
    return outs[0], side.unflatten(outs[1:])


def _pool_kernel(z_ref, prev_ref, next_ref, w_ref, scale_ref, o_ref, ext_ref, *, seq, ts):
    halo = BF16_SUBLANE_TILE
    gd = w_ref.shape[2]
    pos0 = (pl.program_id(0) % (seq // ts)) * ts
    ext_ref[0:halo, :] = jnp.where(pos0 == 0, 0.0, prev_ref[...].astype(F32))
    ext_ref[halo:halo + ts, :] = z_ref[...].astype(F32)
    ext_ref[halo + ts:, :] = jnp.where(pos0 + ts == seq, 0.0, next_ref[...].astype(F32))
    pos = pos0 + lax.broadcasted_iota(jnp.int32, (ts, gd), 0)
    for g, w in enumerate(POOL_WINDOWS):
        cols = slice(g * gd, (g + 1) * gd)
        half = w // 2
        acc = ext_ref[halo - half:halo - half + ts, cols]
        for d in range(-half + 1, half):
            acc = acc + ext_ref[halo + d:halo + d + ts, cols]
        cnt = jnp.minimum(pos + half, seq) - jnp.maximum(pos - half, 0)
        mixed = acc / cnt.astype(F32) - ext_ref[halo:halo + ts, cols]
        y = jnp.dot(mixed.astype(BF16), w_ref[0, g], preferred_element_type=F32)
        o_ref[:, cols] = (y * scale_ref[0, :, cols]).astype(o_ref.dtype)


def _pool(z, pool_w, pool_scale, layer, seq, ts=512):
    T = z.shape[0]
    _, G, gd, _ = pool_w.shape
    width = G * gd
    halo = BF16_SUBLANE_TILE
    n_halo = T // halo
    kern = functools.partial(_pool_kernel, seq=seq, ts=ts)
    return pl.pallas_call(
        kern, grid=(T // ts,),
        in_specs=[
            pl.BlockSpec((ts, width), lambda i: (i, 0)),
            pl.BlockSpec((halo, width), lambda i: (jnp.maximum(i * (ts // halo) - 1, 0), 0)),
            pl.BlockSpec((halo, width), lambda i: (jnp.minimum((i + 1) * (ts // halo), n_halo - 1), 0)),
            pl.BlockSpec((1, G, gd, gd), lambda i: (layer, 0, 0, 0)),
            pl.BlockSpec((1, 1, width), lambda i: (layer, 0, 0)),
        ],
        out_specs=pl.BlockSpec((ts, width), lambda i: (i, 0)),
        out_shape=jax.ShapeDtypeStruct((T, width), BF16),
        scratch_shapes=[pltpu.VMEM((ts + 2 * halo, width), F32)],
        compiler_params=_params("parallel"), name="pool")(z, z, z, pool_w, _rows(pool_scale))


def _retention_kernel(lg_ref, q_ref, k_ref, v_ref, g_ref, cos_ref, sin_ref, gn_ref, o_ref,
                      rs_ref, kr_ref, kvf_ref, *, seq, unroll):
    C = RET_CHUNK
    d = q_ref.shape[1]
    nc = seq // C
    head = pl.program_id(0) % N_RET_HEADS
    lgf = lg_ref[0, head]
    lgb = lg_ref[1, head]
    row = lax.broadcasted_iota(jnp.int32, (C, C), 0).astype(F32)
    col = lax.broadcasted_iota(jnp.int32, (C, C), 1).astype(F32)
    diff = row - col
    decay = jnp.where(diff >= 0, jnp.exp(lgf * jnp.maximum(diff, 0.0)), jnp.exp(lgb * jnp.maximum(-diff, 0.0)))
    rowd = lax.broadcasted_iota(jnp.int32, (C, d), 0).astype(F32)
    qwf = jnp.exp(lgf * (rowd + 1.0))
    qwb = jnp.exp(lgb * (C - rowd))
    kwf = jnp.exp(lgf * (C - 1.0 - rowd))
    kwb = jnp.exp(lgb * rowd)
    chunk_decay_f = jnp.exp(lgf * jnp.full((d, d), float(C), F32))
    chunk_decay_b = jnp.exp(lgb * jnp.full((d, d), float(C), F32))
    tn_dims = (((0,), (0,)), ((), ()))
    nt_dims = (((1,), (1,)), ((), ()))

    def rotary(ref, sl):
        x = ref[sl, :].astype(F32)
        return x * cos_ref[sl, :] + pltpu.roll(x, d // 2, axis=1) * sin_ref[sl, :]

    def chunk(n):
        return pl.ds(pl.multiple_of(n * C, C), C)

    def bwd_body(j, state):
        n = nc - 1 - j
        sl = chunk(n)
        rs_ref[n] = state.astype(BF16)
        k = rotary(k_ref, sl) * (d ** -0.5)
        kr_ref[sl, :] = k.astype(BF16)
        kw = jnp.concatenate([k * kwf, k * kwb], axis=1).astype(BF16)
        kv = lax.dot_general(kw, v_ref[sl, :], tn_dims, preferred_element_type=F32)
        kvf_ref[n] = kv[:d]
        return state * chunk_decay_b + kv[d:]

    lax.fori_loop(0, nc, bwd_body, jnp.zeros((d, d), F32), unroll=unroll)

    def fwd_body(n, state):
        sl = chunk(n)
        qb = rotary(q_ref, sl).astype(BF16)
        v = v_ref[sl, :]
        scores = lax.dot_general(qb, kr_ref[sl, :], nt_dims, preferred_element_type=F32) * decay
        y = jnp.dot(scores.astype(BF16), v, preferred_element_type=F32)
        states = jnp.concatenate([state.astype(BF16), rs_ref[n]], axis=1)
        cross = jnp.dot(qb, states, preferred_element_type=F32)
        y = y + cross[:, :d] * qwf + cross[:, d:] * qwb
        mu = jnp.mean(y, axis=-1, keepdims=True)
        yc = y - mu
        var = jnp.mean(yc * yc, axis=-1, keepdims=True)
        yn = yc * lax.rsqrt(var + LN_EPS) * gn_ref[0]
        gate = g_ref[sl, :].astype(F32)
        o_ref[sl, :] = (yn * (gate * jax.nn.sigmoid(gate))).astype(o_ref.dtype)
        return state * chunk_decay_f + kvf_ref[n]

    lax.fori_loop(0, nc, fwd_body, jnp.zeros((d, d), F32), unroll=unroll)


def _retention(z, log_gamma, gn_g, cos2, sin2, layer, batch, seq, unroll=8):
    T = z.shape[0]
    H = N_RET_HEADS
    d = gn_g.shape[1] // H
    nc = seq // RET_CHUNK

    def col(split):
        return pl.BlockSpec((seq, d), lambda i: (i // H, split * H + i % H))

    table = pl.BlockSpec((seq, d), lambda i: (0, 0))
    kern = functools.partial(_retention_kernel, seq=seq, unroll=unroll)
    return pl.pallas_call(
        kern, grid=(batch * H,),
        in_specs=[pl.BlockSpec(memory_space=pltpu.SMEM), col(1), col(2), col(3), col(4), table, table,
                  pl.BlockSpec((1, 1, d), lambda i: (layer, 0, i % H))],
        out_specs=pl.BlockSpec((seq, d), lambda i: (i // H, i % H)),
        out_shape=jax.ShapeDtypeStruct((T, H * d), BF16),
        scratch_shapes=[pltpu.VMEM((nc, d, d), BF16), pltpu.VMEM((seq, d), BF16), pltpu.VMEM((nc, d, d), F32)],
        compiler_params=_params("parallel"), name="retention")(
            log_gamma[layer], z, z, z, z, cos2, sin2, _rows(gn_g))


def _sgu_kernel(u_ref, v_ref, lng_ref, lnb_ref, w_ref, bias_ref, o_ref, *, chunks):
    P = SGU_CHUNK
    gd = o_ref.shape[1] // N_SGU_GROUPS

    def body(c, _):
        rows = pl.ds(pl.multiple_of(c * P, P), P)
        vn = _layer_norm(jax.nn.gelu(v_ref[rows, :].astype(F32)), lng_ref[0], lnb_ref[0]).astype(BF16)
        for g in range(N_SGU_GROUPS):
            cols = slice(g * gd, (g + 1) * gd)
            mixed = jnp.dot(w_ref[0, g], vn[:, cols], preferred_element_type=F32) + bias_ref[:, cols]
            u = jax.nn.gelu(u_ref[rows, cols].astype(F32))
            o_ref[rows, cols] = (u * mixed).astype(o_ref.dtype)
        return 0

    lax.fori_loop(0, chunks, body, 0)


def _sgu(z, ln_g, ln_b, w_s, b_s, layer, chunks=4):
    T = z.shape[0]
    width = ln_g.shape[1]
    _, G, P, _ = w_s.shape
    rows = chunks * P
    bias = jnp.repeat(b_s[layer].T.astype(F32), width // G, axis=1)
    kern = functools.partial(_sgu_kernel, chunks=chunks)
    vec = pl.BlockSpec((1, 1, width), lambda i: (layer, 0, 0))
    return pl.pallas_call(
        kern, grid=(T // rows,),
        in_specs=[pl.BlockSpec((rows, width), lambda i: (i, 5)), pl.BlockSpec((rows, width), lambda i: (i, 6)),
                  vec, vec, pl.BlockSpec((1, G, P, P), lambda i: (layer, 0, 0, 0)),
                  pl.BlockSpec((P, width), lambda i: (0, 0))],
        out_specs=pl.BlockSpec((rows, width), lambda i: (i, 0)),
        out_shape=jax.ShapeDtypeStruct((T, width), BF16),
        compiler_params=_params("parallel"), name="sgu")(z, z, _rows(ln_g), _rows(ln_b), w_s, bias)


def _mem_attn_kernel(q_ref, kv_ref, o_ref):
    width = o_ref.shape[1]
    hd = width // N_MEM_HEADS
    nt_dims = (((1,), (1,)), ((), ()))
    for h in range(N_MEM_HEADS):
        cols = slice(h * hd, (h + 1) * hd)
        s = lax.dot_general(q_ref[:, cols], kv_ref[:, cols], nt_dims, preferred_element_type=F32) * (hd ** -0.5)
        p = jnp.exp(s - jnp.max(s, axis=-1, keepdims=True))
        p = p / jnp.sum(p, axis=-1, keepdims=True)
        v = kv_ref[:, width + h * hd:width + (h + 1) * hd]
        o_ref[:, cols] = jnp.dot(p.astype(BF16), v, preferred_element_type=F32).astype(o_ref.dtype)


def _mem_attn(z, kv, seq, n_mem, tm=512):
    T = z.shape[0]
    width = kv.shape[1] // 2
    return pl.pallas_call(
        _mem_attn_kernel, grid=(T // tm,),
        in_specs=[pl.BlockSpec((tm, width), lambda i: (i, 7)),
                  pl.BlockSpec((n_mem, 2 * width), lambda i: (i // (seq // tm), 0))],
        out_specs=pl.BlockSpec((tm, width), lambda i: (i, 0)),
        out_shape=jax.ShapeDtypeStruct((T, width), BF16),
        compiler_params=_params("parallel"), name="mem_attn")(z, kv)


def _gate_merge_kernel(hb_ref, y0_ref, y1_ref, y2_ref, y3_ref, wg_ref, bg_ref, wb_ref, *refs, n_side):
    side_in, o_ref, side_out = refs[:n_side], refs[n_side], refs[n_side + 1:]
    acc = None
    for b, y_ref in enumerate((y0_ref, y1_ref, y2_ref, y3_ref)):
        gate = jax.nn.sigmoid(jnp.dot(hb_ref[...], wg_ref[b], preferred_element_type=F32) + bg_ref[0, b:b + 1, :])
        term = gate * jnp.dot(y_ref[...], wb_ref[b], preferred_element_type=F32)
        acc = term if acc is None else acc + term
    o_ref[...] = acc.astype(o_ref.dtype)
    _SideCasts.run(side_in, side_out)


def _gate_merge(hb, ys, w_gate, b_gate, w_branch, layer, side_weights=(), tm=512, tn=512):
    T, D = hb.shape
    bw = w_branch.shape[1]
    n_i = T // tm
    side = _SideCasts(side_weights, layer, (D // tn) * n_i, lambda j, i: j * n_i + i)
    yspec = pl.BlockSpec((tm, bw), lambda j, i: (i, 0))
    once = pl.Buffered(1)
    outs = pl.pallas_call(
        functools.partial(_gate_merge_kernel, n_side=len(side)), grid=(D // tn, n_i),
        in_specs=[pl.BlockSpec((tm, D), lambda j, i: (i, 0)), yspec, yspec, yspec, yspec,
                  pl.BlockSpec((N_BRANCHES, D, tn), lambda j, i: (0, 0, j), pipeline_mode=once),
                  pl.BlockSpec((1, N_BRANCHES, tn), lambda j, i: (layer, 0, j)),
                  pl.BlockSpec((N_BRANCHES, bw, tn), lambda j, i: (0, 0, j), pipeline_mode=once),
                  *side.in_specs],
        out_specs=[pl.BlockSpec((tm, tn), lambda j, i: (i, j)), *side.out_specs],
        out_shape=[jax.ShapeDtypeStruct((T, D), BF16), *side.out_shapes],
        compiler_params=_params("arbitrary", "arbitrary"), name="gate_merge")(
            hb, *ys, w_gate, b_gate, w_branch, *side.arrays)
    return outs[0], side.unflatten(outs[1:])


def _out_ln_kernel(m_ref, w_ref, h_ref, g_ref, b_ref, ho_ref, hbo_ref, hpo_ref, *, alpha):
    tm = m_ref.shape[0]
    mix = jnp.dot(m_ref[...], w_ref[...], preferred_element_type=F32)
    hn = _layer_norm(alpha * h_ref[...] + mix, g_ref[0], b_ref[0])
    ho_ref[...] = hn
    hbo_ref[...] = hn.astype(BF16)
    _store_row_major(hpo_ref, _pack_bf16_pairs(hn), tm)


def _out_ln(merged, w_out, h, g, b, layer, alpha, tm=256):
    T, D = h.shape
    pieces = D // 2 // LANES
    row = pl.BlockSpec((tm, D), lambda i: (i, 0))
    vec = pl.BlockSpec((1, 1, D), lambda i: (layer, 0, 0))
    kern = functools.partial(_out_ln_kernel, alpha=alpha)
    return pl.pallas_call(
        kern, grid=(T // tm,),
        in_specs=[row, pl.BlockSpec((D, D), lambda i: (0, 0)), row, vec, vec],
        out_specs=[row, row, pl.BlockSpec((tm * pieces, LANES), lambda i: (i, 0))],
        out_shape=[jax.ShapeDtypeStruct((T, D), F32), jax.ShapeDtypeStruct((T, D), BF16),
                   jax.ShapeDtypeStruct((T * pieces, LANES), U32)],
        compiler_params=_params("parallel"), name="out_ln1")(merged, w_out, h, _rows(g), _rows(b))


def _router_kernel(hb_ref, wr_ref, br_ref, e_ref, gate_ref, rank_ref, cnt_ref, carry_ref):
    tm = hb_ref.shape[0]

    @pl.when(pl.program_id(0) == 0)
    def _():
        carry_ref[...] = jnp.zeros_like(carry_ref)

    nt_dims = (((1,), (1,)), ((), ()))
    logits = lax.dot_general(wr_ref[...], hb_ref[...], nt_dims, preferred_element_type=F32) + br_ref[...]
    eidx = lax.broadcasted_iota(jnp.int32, logits.shape, 0)
    work = logits
    tops, hots = [], []
    for k in range(TOP_K):
        m = jnp.max(work, axis=0, keepdims=True)
        sel = jnp.min(jnp.where(work == m, eidx, N_EXPERTS), axis=0, keepdims=True)
        hot = eidx == sel
        work = jnp.where(hot, -jnp.inf, work)
        e_ref[k:k + 1, :] = sel
        tops.append(m)
        hots.append(hot)
    ex = [jnp.exp(t - tops[0]) for t in tops]
    den = ex[0] + ex[1] + ex[2] + ex[3]
    member = jnp.zeros(logits.shape, F32)
    for hot in hots:
        member = member + jnp.where(hot, 1.0, 0.0)
    tri = jnp.where(lax.broadcasted_iota(jnp.int32, (tm, tm), 0) <= lax.broadcasted_iota(jnp.int32, (tm, tm), 1),
                    1.0, 0.0).astype(BF16)
    incl = jnp.dot(member.astype(BF16), tri, preferred_element_type=F32)
    before = carry_ref[:, 0:1] + incl - member
    for k in range(TOP_K):
        gate_ref[k:k + 1, :] = ex[k] / den
        rank_ref[k:k + 1, :] = jnp.sum(jnp.where(hots[k], before, 0.0), axis=0, keepdims=True).astype(jnp.int32)
    carry_ref[...] = carry_ref[...] + incl[:, tm - 1:tm]
    cnt_ref[...] = carry_ref[...].astype(jnp.int32)


def _router(hb, w_router_t, b_router, tm=512):
    T, D = hb.shape
    E = w_router_t.shape[0]
    tok = pl.BlockSpec((TOP_K, tm), lambda i: (0, i))
    return pl.pallas_call(
        _router_kernel, grid=(T // tm,),
        in_specs=[pl.BlockSpec((tm, D), lambda i: (i, 0)), pl.BlockSpec((E, D), lambda i: (0, 0)),
                  pl.BlockSpec((E, 1), lambda i: (0, 0))],
        out_specs=[tok, tok, tok, pl.BlockSpec((E, 128), lambda i: (0, 0))],
        out_shape=[jax.ShapeDtypeStruct((TOP_K, T), jnp.int32), jax.ShapeDtypeStruct((TOP_K, T), F32),
                   jax.ShapeDtypeStruct((TOP_K, T), jnp.int32), jax.ShapeDtypeStruct((E, 128), jnp.int32)],
        scratch_shapes=[pltpu.VMEM((E, 128), F32)],
        compiler_params=_params("arbitrary"), name="router")(hb, w_router_t, b_router.reshape(E, 1))


def _dispatch_kernel(plo_ref, phi_ref, dst_hbm, hp_ref, xs_hbm, idx_ref, zero_ref, idx_sem, row_sem, zero_sem,
                     *, n_tiles, n_blocks, n):
    tm = hp_ref.shape[0] // n
    blk_rows = zero_ref.shape[0]
    bm = blk_rows // n
    i = pl.program_id(0)

    def idx_copy(tile):
        return pltpu.make_async_copy(dst_hbm.at[tile], idx_ref, idx_sem)

    def zero_row(r):
        return pltpu.make_async_copy(zero_ref.at[pl.ds(0, n)], xs_hbm.at[pl.ds(pl.multiple_of(r * n, n), n)], zero_sem)

    def zero_block(blk):
        return pltpu.make_async_copy(
            zero_ref, xs_hbm.at[pl.ds(pl.multiple_of(blk * blk_rows, blk_rows), blk_rows)], zero_sem)

    def for_each(lo, hi, fn):
        def body(r, _):
            fn(r)
            return 0
        lax.fori_loop(lo, hi, body, 0)

    @pl.when(i == 0)
    def _():
        idx_copy(0).start()
        zero_ref[...] = jnp.zeros_like(zero_ref)
        for e in range(N_EXPERTS):
            for_each(plo_ref[e], phi_ref[e], lambda r: zero_row(r).start())
            for_each(plo_ref[e], phi_ref[e], lambda r: zero_row(r).wait())
        first_unused = phi_ref[N_EXPERTS - 1] // bm
        for_each(first_unused, n_blocks, lambda blk: zero_block(blk).start())
        for_each(first_unused, n_blocks, lambda blk: zero_block(blk).wait())

    idx_copy(i).wait()
    for r0 in range(0, tm, ISSUE_BATCH_TOKENS):
        batch = [(r, k, idx_ref[r * TOP_K + k]) for r in range(r0, r0 + ISSUE_BATCH_TOKENS) for k in range(TOP_K)]
        for r, k, d in batch:
            d = pl.multiple_of(d, n)
            pltpu.make_async_copy(hp_ref.at[pl.ds(r * n, n)], xs_hbm.at[pl.ds(d, n)], row_sem).start(priority=k % 2)

    @pl.when(i + 1 < n_tiles)
    def _():
        idx_copy(i + 1).start()

    for k in range(TOP_K):
        pltpu.make_async_copy(hp_ref, xs_hbm.at[pl.ds(0, tm * n)], row_sem).wait()


def _dispatch(hp, dest_tiles, pad_lo, pad_hi, n_slots, n):
    n_tiles, ktm = dest_tiles.shape
    tm = ktm // TOP_K
    kern = functools.partial(_dispatch_kernel, n_tiles=n_tiles, n_blocks=n_slots // MOE_BLOCK, n=n)
    grid_spec = pltpu.PrefetchScalarGridSpec(
        num_scalar_prefetch=2, grid=(n_tiles,),
        in_specs=[pl.BlockSpec(memory_space=pl.ANY), pl.BlockSpec((tm * n, LANES), lambda i, lo, hi: (i, 0))],
        out_specs=pl.BlockSpec(memory_space=pl.ANY),
        scratch_shapes=[pltpu.SMEM((ktm,), jnp.int32), pltpu.VMEM((MOE_BLOCK * n, LANES), U32),
                        pltpu.SemaphoreType.DMA, pltpu.SemaphoreType.DMA, pltpu.SemaphoreType.DMA])
    return pl.pallas_call(
        kern, grid_spec=grid_spec, out_shape=jax.ShapeDtypeStruct((n_slots * n, LANES), U32),
        compiler_params=_params("arbitrary"), name="moe_dispatch")(pad_lo, pad_hi, dest_tiles, hp)


def _moe_kernel(be_ref, bn_ref, x_ref, wup_ref, bup_ref, wdn_ref, bdn_ref, y_ref, *, bm):
    ed = wdn_ref.shape[1]
    n_x = x_ref.shape[0] // bm
    i = pl.program_id(0)

    @pl.when(bn_ref[i] > 0)
    def _():
        x = _unpack_bf16_pairs(jnp.concatenate(_load_row_major(x_ref, bm, n_x), axis=1))
        gu = jnp.dot(x, wup_ref[0], preferred_element_type=F32) + bup_ref[0]
        glu = jnp.minimum(gu[:, :ed], SWIGLU_LIMIT)
        lin = jnp.clip(gu[:, ed:], -SWIGLU_LIMIT, SWIGLU_LIMIT)
        act = glu * jax.nn.sigmoid(SWIGLU_ALPHA * glu) * (lin + 1.0)
        y = jnp.dot(act.astype(BF16), wdn_ref[0], preferred_element_type=F32) + bdn_ref[0]
        _store_row_major(y_ref, y, bm)

    @pl.when(bn_ref[i] == 0)
    def _():
        y_ref[...] = jnp.zeros_like(y_ref)


def _moe(xs, blk_e, blk_n, w_up, b_up, w_down, b_down, layer, n_x):
    bm = MOE_BLOCK
    n_blocks = xs.shape[0] // (bm * n_x)
    _, D, two_ed = w_up.shape
    ed = two_ed // 2
    n_y = D // LANES
    base = layer * N_EXPERTS

    def expert(i, be, bn):
        return (be[i], 0, 0)

    def expert_bias(i, be, bn):
        return (base + be[i], 0, 0)

    grid_spec = pltpu.PrefetchScalarGridSpec(
        num_scalar_prefetch=2, grid=(n_blocks,),
        in_specs=[pl.BlockSpec((bm * n_x, LANES), lambda i, be, bn: (jnp.where(bn[i] > 0, i, 0), 0)),
                  pl.BlockSpec((1, D, two_ed), expert), pl.BlockSpec((1, 1, two_ed), expert_bias),
                  pl.BlockSpec((1, ed, D), expert), pl.BlockSpec((1, 1, D), expert_bias)],
        out_specs=pl.BlockSpec((bm * n_y, LANES), lambda i, be, bn: (i, 0)))
    return pl.pallas_call(
        functools.partial(_moe_kernel, bm=bm), grid_spec=grid_spec,
        out_shape=jax.ShapeDtypeStruct((n_blocks * bm * n_y, LANES), F32),
        compiler_params=_params("arbitrary"), name="moe_experts")(blk_e, blk_n, xs, w_up, b_up, w_down, b_down)


def _combine_kernel(dst_hbm, y_hbm, gates_ref, h_ref, g_ref, b_ref, ho_ref, hbo_ref,
                    idx_ref, ybuf, idx_sem, row_sem, *, n_tiles, alpha):
    tm, D = h_ref.shape
    n = D // LANES
    i = pl.program_id(0)
    slot = i % 2

    def idx_copy(tile):
        return pltpu.make_async_copy(dst_hbm.at[tile], idx_ref, idx_sem)

    @pl.when(i == 0)
    def _():
        idx_copy(0).start()

    @pl.when(i < n_tiles)
    def _():
        idx_copy(i).wait()
        for r0 in range(0, tm, ISSUE_BATCH_TOKENS):
            batch = [(r, k, idx_ref[r * TOP_K + k]) for r in range(r0, r0 + ISSUE_BATCH_TOKENS) for k in range(TOP_K)]
            for r, k, row in batch:
                row = pl.multiple_of(row, n)
                pltpu.make_async_copy(y_hbm.at[pl.ds(row, n)], ybuf.at[slot, k, pl.ds(r * n, n)],
                                      row_sem.at[slot]).start(priority=k % 2)

    @pl.when(i + 1 < n_tiles)
    def _():
        idx_copy(i + 1).start()

    @pl.when(i > 0)
    def _():
        prev = 1 - slot
        pltpu.make_async_copy(ybuf.at[prev], ybuf.at[prev], row_sem.at[prev]).wait()
        gate = [gates_ref[:, k:k + 1] for k in range(TOP_K)]
        pieces = None
        for k in range(TOP_K):
            yk = _load_row_major(ybuf, tm, n, index=(prev, k))
            pieces = [gate[k] * p for p in yk] if pieces is None else [a + gate[k] * p for a, p in zip(pieces, yk)]
        ffn = jnp.concatenate(pieces, axis=1)
        hn = _layer_norm(alpha * h_ref[...] + ffn, g_ref[0], b_ref[0])
        ho_ref[...] = hn
        hbo_ref[...] = hn.astype(BF16)


def _combine(dest_tiles, y_sorted, gates_t, h, g, b, layer, alpha):
    T, D = h.shape
    n_tiles, ktm = dest_tiles.shape
    tm = ktm // TOP_K

    def tile(i):
        return (jnp.maximum(i - 1, 0), 0)

    row = pl.BlockSpec((tm, D), tile)
    vec = pl.BlockSpec((1, 1, D), lambda i: (layer, 0, 0))
    kern = functools.partial(_combine_kernel, n_tiles=n_tiles, alpha=alpha)
    return pl.pallas_call(
        kern, grid=(n_tiles + 1,),
        in_specs=[pl.BlockSpec(memory_space=pl.ANY), pl.BlockSpec(memory_space=pl.ANY),
                  pl.BlockSpec((tm, TOP_K), tile), row, vec, vec],
        out_specs=[row, row],
        out_shape=[jax.ShapeDtypeStruct((T, D), F32), jax.ShapeDtypeStruct((T, D), BF16)],
        scratch_shapes=[pltpu.SMEM((ktm,), jnp.int32), pltpu.VMEM((2, TOP_K, tm * (D // LANES), LANES), F32),
                        pltpu.SemaphoreType.DMA, pltpu.SemaphoreType.DMA((2,))],
        compiler_params=_params("arbitrary"), name="moe_combine_ln2")(
            dest_tiles, y_sorted, gates_t, h, _rows(g), _rows(b))


def _routing_plan(top_e, rank, counts, n_blocks):
    T = top_e.shape[1]
    bm = MOE_BLOCK
    experts = jnp.arange(N_EXPERTS, dtype=jnp.int32)
    padded = (counts + bm - 1) // bm * bm
    pad_end = jnp.cumsum(padded)
    pad_start = pad_end - padded
    start_of = jnp.sum(jnp.where(top_e[None] == experts[:, None, None], pad_start[:, None, None], 0), axis=0)
    dest = start_of + rank
    blk_start = jnp.arange(n_blocks, dtype=jnp.int32) * bm
    blk_e = jnp.minimum(jnp.sum(pad_end[None, :] <= blk_start[:, None], axis=1), N_EXPERTS - 1).astype(jnp.int32)
    hot = blk_e[:, None] == experts[None, :]
    blk_cnt = jnp.sum(jnp.where(hot, counts[None, :], 0), axis=1)
    blk_pad_start = jnp.sum(jnp.where(hot, pad_start[None, :], 0), axis=1)
    blk_n = jnp.clip(blk_cnt - (blk_start - blk_pad_start), 0, bm).astype(jnp.int32)
    tm = ROUTE_TILE
    dest_tiles = dest.T.reshape(T // tm, tm * TOP_K).astype(jnp.int32)
    return dest_tiles, blk_e, blk_n, (pad_start + counts).astype(jnp.int32), pad_end.astype(jnp.int32)


def kernel(x, mem, ln_in_g, ln_in_b, w_in, pool_w, pool_scale, ret_log_gamma, ret_gn_g, sgu_ln_g, sgu_ln_b, sgu_w, sgu_b, w_mem_kv, w_branch, w_gate, b_gate, w_out, ln1_g, ln1_b, w_router, b_router, w_up, b_up, w_down, b_down, ln2_g, ln2_b):
    B, S, D = x.shape
    depth = w_in.shape[0]
    n_mem = mem.shape[1]
    T = B * S
    alpha = (2 * depth) ** 0.25
    head_dim = ret_gn_g.shape[1] // N_RET_HEADS
    n_blocks = -(-(T * TOP_K + N_EXPERTS * (MOE_BLOCK - 1)) // MOE_BLOCK)

    pos = jnp.arange(S, dtype=F32)
    inv_freq = jnp.exp(-math.log(ROPE_BASE) * jnp.arange(0, head_dim, 2, dtype=F32) / head_dim)
    ang = pos[:, None] * inv_freq[None, :]
    cos, sin = jnp.cos(ang), jnp.sin(ang)
    cos2 = jnp.concatenate([cos, cos], axis=1)
    sin2 = jnp.concatenate([-sin, sin], axis=1)

    pool_wb, sgu_wb = pool_w.astype(BF16), sgu_w.astype(BF16)
    w_router_t = jnp.swapaxes(w_router, 1, 2).astype(BF16)
    E = w_up.shape[1]
    b_up_r = b_up.reshape(depth * E, 1, -1)
    b_down_r = b_down.reshape(depth * E, 1, D)
    n_x = D // 2 // LANES
    n_y = D // LANES

    mem_b = mem.reshape(B * n_mem, D).astype(BF16)
    h, hb = _ln_in(x.reshape(T, D), ln_in_g, ln_in_b)
    for l in range(depth):
        z, (w_gate_b, w_branch_b, w_out_b) = _matmul(hb, w_in, l, tm=1024, tn=1024, name="in_proj",
                                                     side_weights=(w_gate, w_branch, w_out))
        kv, _ = _matmul(mem_b, w_mem_kv, l, tm=B * n_mem, tn=1024, name="mem_kv")
        ys = (
            _pool(z, pool_wb, pool_scale, l, S),
            _retention(z, ret_log_gamma, ret_gn_g, cos2, sin2, l, B, S),
            _sgu(z, sgu_ln_g, sgu_ln_b, sgu_wb, sgu_b, l),
            _mem_attn(z, kv, S, n_mem),
        )
        merged, (w_up_b, w_down_b) = _gate_merge(hb, ys, w_gate_b, b_gate, w_branch_b, l,
                                                 side_weights=(w_up, w_down))
        h, hb, hp = _out_ln(merged, w_out_b, h, ln1_g, ln1_b, l, alpha)

        top_e, gates, rank, cnt = _router(hb, w_router_t[l], b_router[l])
        dest_tiles, blk_e, blk_n, pad_lo, pad_hi = _routing_plan(top_e, rank, cnt[:, 0], n_blocks)
        xs = _dispatch(hp, dest_tiles * n_x, pad_lo, pad_hi, n_blocks * MOE_BLOCK, n_x)
        y_sorted = _moe(xs, blk_e, blk_n, w_up_b, b_up_r, w_down_b, b_down_r, l, n_x)
        h, hb = _combine(dest_tiles * n_y, y_sorted, gates.T, h, ln2_g, ln2_b, l, alpha)
    return h.reshape(B, S, D)
```

```python
import functools
import math

import jax
import jax.numpy as jnp
from jax import lax
from jax.experimental import pallas as pl
from jax.experimental.pallas import tpu as pltpu

F32 = jnp.float32
BF16 = jnp.bfloat16
U32 = jnp.uint32

N_BRANCHES = 4
POOL_WINDOWS = (2, 4, 8, 16)
N_RET_HEADS = 8
RET_CHUNK = 128
ROPE_BASE = 10000.0
SGU_CHUNK = 128
N_SGU_GROUPS = 4
N_MEM_HEADS = 4
N_EXPERTS = 32
TOP_K = 4
SWIGLU_LIMIT = 7.0
SWIGLU_ALPHA = 1.702
LN_EPS = 1e-5

V7X_VMEM_BYTES = 64 * 1024 * 1024
VMEM_LIMIT_BYTES = V7X_VMEM_BYTES - 8 * 1024 * 1024
BF16_SUBLANE_TILE = 16
LANES = 128

MOE_BLOCK = 256
ROUTE_TILE = 128

def _params(*semantics):
    return pltpu.CompilerParams(dimension_semantics=semantics, vmem_limit_bytes=VMEM_LIMIT_BYTES)


def _rows(v):
    return v.reshape(v.shape[0], 1, v.shape[1])


def _layer_norm(x, g, b):
    mu = jnp.mean(x, axis=-1, keepdims=True)
    xc = x - mu
    var = jnp.mean(xc * xc, axis=-1, keepdims=True)
    return xc * lax.rsqrt(var + LN_EPS) * g + b


def _pack_bf16_pairs(x):
    n = x.shape[1] // 2
    xb = x.astype(BF16).astype(F32)
    lo = lax.bitcast_convert_type(xb[:, :n], U32)
    hi = lax.bitcast_convert_type(xb[:, n:], U32)
    return (lo >> 16) | hi


def _row_pitch(n):
    assert n % 2 == 0
    return n + 1


def _store_row_major(ref, x, rows):
    n = x.shape[1] // LANES
    pitch = _row_pitch(n)
    for j in range(n):
        ref[pl.ds(j, rows, stride=pitch), :] = x[:, j * LANES:(j + 1) * LANES]
    ref[pl.ds(n, rows, stride=pitch), :] = jnp.zeros((rows, LANES), ref.dtype)


def _load_row_major(ref, rows, n, index=()):
    return [ref[(*index, pl.ds(j, rows, stride=_row_pitch(n)), slice(None))] for j in range(n)]


def _unpack_bf16_pairs(p):
    lo = lax.bitcast_convert_type(p << 16, F32).astype(BF16)
    hi = lax.bitcast_convert_type(p & jnp.uint32(0xFFFF0000), F32).astype(BF16)
    return jnp.concatenate([lo, hi], axis=1)


def _ln_in_kernel(x_ref, g_ref, b_ref, h_ref, hb_ref):
    h = _layer_norm(x_ref[...], g_ref[...], b_ref[...])
    h_ref[...] = h
    hb_ref[...] = h.astype(BF16)


def _ln_in(x, g, b, tm=512):
    T, D = x.shape
    row = pl.BlockSpec((tm, D), lambda i: (i, 0))
    vec = pl.BlockSpec((1, D), lambda i: (0, 0))
    return pl.pallas_call(
        _ln_in_kernel, grid=(T // tm,), in_specs=[row, vec, vec], out_specs=[row, row],
        out_shape=[jax.ShapeDtypeStruct((T, D), F32), jax.ShapeDtypeStruct((T, D), BF16)],
        compiler_params=_params("parallel"), name="ln_in")(x, g.reshape(1, D), b.reshape(1, D))


class _SideCasts:
    def __init__(self, weights, layer, n_steps, step_of):
        self.arrays, self.in_specs, self.out_specs, self.out_shapes, self.shapes = [], [], [], [], []
        for w in weights:
            cols = w.shape[-1]
            rows = math.prod(w.shape[1:-1])
            slab = rows // n_steps
            assert slab * n_steps == rows and slab % BF16_SUBLANE_TILE == 0, (w.shape, n_steps)
            self.arrays.append(w.reshape(w.shape[0] * rows, cols))
            self.in_specs.append(pl.BlockSpec((slab, cols), lambda *g: (layer * n_steps + step_of(*g), 0)))
            self.out_specs.append(pl.BlockSpec((slab, cols), lambda *g: (step_of(*g), 0)))
            self.out_shapes.append(jax.ShapeDtypeStruct((rows, cols), BF16))
            self.shapes.append(w.shape[1:])

    def __len__(self):
        return len(self.arrays)

    @staticmethod
    def run(in_refs, out_refs):
        for i_ref, o_ref in zip(in_refs, out_refs):
            o_ref[...] = i_ref[...].astype(BF16)

    def unflatten(self, outs):
        return [o.reshape(s) for o, s in zip(outs, self.shapes)]


def _matmul_kernel(a_ref, w_ref, *refs, n_side):
    side_in, o_ref, side_out, wb_ref = refs[:n_side], refs[n_side], refs[n_side + 1:-1], refs[-1]

    @pl.when(pl.program_id(1) == 0)
    def _():
        wb_ref[...] = w_ref[0].astype(BF16)

    o_ref[...] = jnp.dot(a_ref[...], wb_ref[...], preferred_element_type=F32).astype(o_ref.dtype)
    _SideCasts.run(side_in, side_out)


def _matmul(a, w, layer, tm, tn, name, side_weights=()):
    M, K = a.shape
    N = w.shape[2]
    n_i = M // tm
    side = _SideCasts(side_weights, layer, (N // tn) * n_i, lambda j, i: j * n_i + i)
    outs = pl.pallas_call(
        functools.partial(_matmul_kernel, n_side=len(side)), grid=(N // tn, n_i),
        in_specs=[pl.BlockSpec((tm, K), lambda j, i: (i, 0)), pl.BlockSpec((1, K, tn), lambda j, i: (layer, 0, j)),
                  *side.in_specs],
        out_specs=[pl.BlockSpec((tm, tn), lambda j, i: (i, j)), *side.out_specs],
        out_shape=[jax.ShapeDtypeStruct((M, N), BF16), *side.out_shapes],
        scratch_shapes=[pltpu.VMEM((K, tn), BF16)],
        compiler_params=_params("arbitrary", "arbitrary"), name=name)(a, w, *side.arrays)
    return outs[0], side.unflatten(outs[1:])


def _pool_kernel(z_ref, prev_ref, next_ref, w_ref, scale_ref, o_ref, ext_ref, *, seq, ts):
    halo = BF16_SUBLANE_TILE
    gd = w_ref.shape[2]
    pos0 = (pl.program_id(0) % (seq // ts)) * ts
    ext_ref[0:halo, :] = jnp.where(pos0 == 0, 0.0, prev_ref[...].astype(F32))
    ext_ref[halo:halo + ts, :] = z_ref[...].astype(F32)
    ext_ref[halo + ts:, :] = jnp.where(pos0 + ts == seq, 0.0, next_ref[...].astype(F32))
    pos = pos0 + lax.broadcasted_iota(jnp.int32, (ts, gd), 0)
    for g, w in enumerate(POOL_WINDOWS):
        cols = slice(g * gd, (g + 1) * gd)
        half = w // 2
        acc = ext_ref[halo - half:halo - half + ts, cols]
        for d in range(-half + 1, half):
            acc = acc + ext_ref[halo + d:halo + d + ts, cols]
        cnt = jnp.minimum(pos + half, seq) - jnp.maximum(pos - half, 0)
        mixed = acc / cnt.astype(F32) - ext_ref[halo:halo + ts, cols]
        y = jnp.dot(mixed.astype(BF16), w_ref[0, g], preferred_element_type=F32)
        o_ref[:, cols] = (y * scale_ref[0, :, cols]).astype(o_ref.dtype)


def _pool(z, pool_w, pool_scale, layer, seq, ts=512):
    T = z.shape[0]
    _, G, gd, _ = pool_w.shape
    width = G * gd
    halo = BF16_SUBLANE_TILE
    n_halo = T // halo
    kern = functools.partial(_pool_kernel, seq=seq, ts=ts)
    return pl.pallas_call(
        kern, grid=(T // ts,),
        in_specs=[
            pl.BlockSpec((ts, width), lambda i: (i, 0)),
            pl.BlockSpec((halo, width), lambda i: (jnp.maximum(i * (ts // halo) - 1, 0), 0)),
            pl.BlockSpec((halo, width), lambda i: (jnp.minimum((i + 1) * (ts // halo), n_halo - 1), 0)),
            pl.BlockSpec((1, G, gd, gd), lambda i: (layer, 0, 0, 0)),
            pl.BlockSpec((1, 1, width), lambda i: (layer, 0, 0)),
        ],
        out_specs=pl.BlockSpec((ts, width), lambda i: (i, 0)),
        out_shape=jax.ShapeDtypeStruct((T, width), BF16),
        scratch_shapes=[pltpu.VMEM((ts + 2 * halo, width), F32)],
        compiler_params=_params("parallel"), name="pool")(z, z, z, pool_w, _rows(pool_scale))


def _retention_kernel(lg_ref, q_ref, k_ref, v_ref, g_ref, cos_ref, sin_ref, gn_ref, o_ref,
                      rs_ref, kr_ref, kvf_ref, *, seq, unroll):
    C = RET_CHUNK
    d = q_ref.shape[1]
    nc = seq // C
    head = pl.program_id(0) % N_RET_HEADS
    lgf = lg_ref[0, head]
    lgb = lg_ref[1, head]
    row = lax.broadcasted_iota(jnp.int32, (C, C), 0).astype(F32)
    col = lax.broadcasted_iota(jnp.int32, (C, C), 1).astype(F32)
    diff = row - col
    decay = jnp.where(diff >= 0, jnp.exp(lgf * jnp.maximum(diff, 0.0)), jnp.exp(lgb * jnp.maximum(-diff, 0.0)))
    rowd = lax.broadcasted_iota(jnp.int32, (C, d), 0).astype(F32)
    qwf = jnp.exp(lgf * (rowd + 1.0))
    qwb = jnp.exp(lgb * (C - rowd))
    kwf = jnp.exp(lgf * (C - 1.0 - rowd))
    kwb = jnp.exp(lgb * rowd)
    chunk_decay_f = jnp.exp(lgf * jnp.full((d, d), float(C), F32))
    chunk_decay_b = jnp.exp(lgb * jnp.full((d, d), float(C), F32))
    tn_dims = (((0,), (0,)), ((), ()))
    nt_dims = (((1,), (1,)), ((), ()))

    def rotary(ref, sl):
        x = ref[sl, :].astype(F32)
        return x * cos_ref[sl, :] + pltpu.roll(x, d // 2, axis=1) * sin_ref[sl, :]

    def chunk(n):
        return pl.ds(pl.multiple_of(n * C, C), C)

    def bwd_body(j, state):
        n = nc - 1 - j
        sl = chunk(n)
        rs_ref[n] = state.astype(BF16)
        k = rotary(k_ref, sl) * (d ** -0.5)
        kr_ref[sl, :] = k.astype(BF16)
        kw = jnp.concatenate([k * kwf, k * kwb], axis=1).astype(BF16)
        kv = lax.dot_general(kw, v_ref[sl, :], tn_dims, preferred_element_type=F32)
        kvf_ref[n] = kv[:d]
        return state * chunk_decay_b + kv[d:]

    lax.fori_loop(0, nc, bwd_body, jnp.zeros((d, d), F32), unroll=unroll)

    def fwd_body(n, state):
        sl = chunk(n)
        qb = rotary(q_ref, sl).astype(BF16)
        v = v_ref[sl, :]
        scores = lax.dot_general(qb, kr_ref[sl, :], nt_dims, preferred_element_type=F32) * decay
        y = jnp.dot(scores.astype(BF16), v, preferred_element_type=F32)
        states = jnp.concatenate([state.astype(BF16), rs_ref[n]], axis=1)
        cross = jnp.dot(qb, states, preferred_element_type=F32)
        y = y + cross[:, :d] * qwf + cross[:, d:] * qwb
        mu = jnp.mean(y, axis=-1, keepdims=True)
        yc = y - mu
        var = jnp.mean(yc * yc, axis=-1, keepdims=True)
        yn = yc * lax.rsqrt(var + LN_EPS) * gn_ref[0]
        gate = g_ref[sl, :].astype(F32)
        o_ref[sl, :] = (yn * (gate * jax.nn.sigmoid(gate))).astype(o_ref.dtype)
        return state * chunk_decay_f + kvf_ref[n]

    lax.fori_loop(0, nc, fwd_body, jnp.zeros((d, d), F32), unroll=unroll)


def _retention(z, log_gamma, gn_g, cos2, sin2, layer, batch, seq, unroll=8):
    T = z.shape[0]
    H = N_RET_HEADS
    d = gn_g.shape[1] // H
    nc = seq // RET_CHUNK

    def col(split):
        return pl.BlockSpec((seq, d), lambda i: (i // H, split * H + i % H))

    table = pl.BlockSpec((seq, d), lambda i: (0, 0))
    kern = functools.partial(_retention_kernel, seq=seq, unroll=unroll)
    return pl.pallas_call(
        kern, grid=(batch * H,),
        in_specs=[pl.BlockSpec(memory_space=pltpu.SMEM), col(1), col(2), col(3), col(4), table, table,
                  pl.BlockSpec((1, 1, d), lambda i: (layer, 0, i % H))],
        out_specs=pl.BlockSpec((seq, d), lambda i: (i // H, i % H)),
        out_shape=jax.ShapeDtypeStruct((T, H * d), BF16),
        scratch_shapes=[pltpu.VMEM((nc, d, d), BF16), pltpu.VMEM((seq, d), BF16), pltpu.VMEM((nc, d, d), F32)],
        compiler_params=_params("parallel"), name="retention")(
            log_gamma[layer], z, z, z, z, cos2, sin2, _rows(gn_g))


def _sgu_kernel(u_ref, v_ref, lng_ref, lnb_ref, w_ref, bias_ref, o_ref, *, chunks):
    P = SGU_CHUNK
    gd = o_ref.shape[1] // N_SGU_GROUPS

    def body(c, _):
        rows = pl.ds(pl.multiple_of(c * P, P), P)
        vn = _layer_norm(jax.nn.gelu(v_ref[rows, :].astype(F32)), lng_ref[0], lnb_ref[0]).astype(BF16)
        for g in range(N_SGU_GROUPS):
            cols = slice(g * gd, (g + 1) * gd)
            mixed = jnp.dot(w_ref[0, g], vn[:, cols], preferred_element_type=F32) + bias_ref[:, cols]
            u = jax.nn.gelu(u_ref[rows, cols].astype(F32))
            o_ref[rows, cols] = (u * mixed).astype(o_ref.dtype)
        return 0

    lax.fori_loop(0, chunks, body, 0)


def _sgu(z, ln_g, ln_b, w_s, b_s, layer, chunks=4):
    T = z.shape[0]
    width = ln_g.shape[1]
    _, G, P, _ = w_s.shape
    rows = chunks * P
    bias = jnp.repeat(b_s[layer].T.astype(F32), width // G, axis=1)
    kern = functools.partial(_sgu_kernel, chunks=chunks)
    vec = pl.BlockSpec((1, 1, width), lambda i: (layer, 0, 0))
    return pl.pallas_call(
        kern, grid=(T // rows,),
        in_specs=[pl.BlockSpec((rows, width), lambda i: (i, 5)), pl.BlockSpec((rows, width), lambda i: (i, 6)),
                  vec, vec, pl.BlockSpec((1, G, P, P), lambda i: (layer, 0, 0, 0)),
                  pl.BlockSpec((P, width), lambda i: (0, 0))],
        out_specs=pl.BlockSpec((rows, width), lambda i: (i, 0)),
        out_shape=jax.ShapeDtypeStruct((T, width), BF16),
        compiler_params=_params("parallel"), name="sgu")(z, z, _rows(ln_g), _rows(ln_b), w_s, bias)


def _mem_attn_kernel(q_ref, kv_ref, o_ref):
    width = o_ref.shape[1]
    hd = width // N_MEM_HEADS
    nt_dims = (((1,), (1,)), ((), ()))
    for h in range(N_MEM_HEADS):
        cols = slice(h * hd, (h + 1) * hd)
        s = lax.dot_general(q_ref[:, cols], kv_ref[:, cols], nt_dims, preferred_element_type=F32) * (hd ** -0.5)
        p = jnp.exp(s - jnp.max(s, axis=-1, keepdims=True))
        p = p / jnp.sum(p, axis=-1, keepdims=True)
        v = kv_ref[:, width + h * hd:width + (h + 1) * hd]
        o_ref[:, cols] = jnp.dot(p.astype(BF16), v, preferred_element_type=F32).astype(o_ref.dtype)


def _mem_attn(z, kv, seq, n_mem, tm=512):
    T = z.shape[0]
    width = kv.shape[1] // 2
    return pl.pallas_call(
        _mem_attn_kernel, grid=(T // tm,),
        in_specs=[pl.BlockSpec((tm, width), lambda i: (i, 7)),
                  pl.BlockSpec((n_mem, 2 * width), lambda i: (i // (seq // tm), 0))],
        out_specs=pl.BlockSpec((tm, width), lambda i: (i, 0)),
        out_shape=jax.ShapeDtypeStruct((T, width), BF16),
        compiler_params=_params("parallel"), name="mem_attn")(z, kv)


def _gate_merge_kernel(hb_ref, y0_ref, y1_ref, y2_ref, y3_ref, wg_ref, bg_ref, wb_ref, *refs, n_side):
    side_in, o_ref, side_out = refs[:n_side], refs[n_side], refs[n_side + 1:]
    acc = None
    for b, y_ref in enumerate((y0_ref, y1_ref, y2_ref, y3_ref)):
        gate = jax.nn.sigmoid(jnp.dot(hb_ref[...], wg_ref[b], preferred_element_type=F32) + bg_ref[0, b:b + 1, :])
        term = gate * jnp.dot(y_ref[...], wb_ref[b], preferred_element_type=F32)
        acc = term if acc is None else acc + term
    o_ref[...] = acc.astype(o_ref.dtype)
    _SideCasts.run(side_in, side_out)


def _gate_merge(hb, ys, w_gate, b_gate, w_branch, layer, side_weights=(), tm=512, tn=512):
    T, D = hb.shape
    bw = w_branch.shape[1]
    n_i = T // tm
    side = _SideCasts(side_weights, layer, (D // tn) * n_i, lambda j, i: j * n_i + i)
    yspec = pl.BlockSpec((tm, bw), lambda j, i: (i, 0))
    once = pl.Buffered(1)
    outs = pl.pallas_call(
        functools.partial(_gate_merge_kernel, n_side=len(side)), grid=(D // tn, n_i),
        in_specs=[pl.BlockSpec((tm, D), lambda j, i: (i, 0)), yspec, yspec, yspec, yspec,
                  pl.BlockSpec((N_BRANCHES, D, tn), lambda j, i: (0, 0, j), pipeline_mode=once),
                  pl.BlockSpec((1, N_BRANCHES, tn), lambda j, i: (layer, 0, j)),
                  pl.BlockSpec((N_BRANCHES, bw, tn), lambda j, i: (0, 0, j), pipeline_mode=once),
                  *side.in_specs],
        out_specs=[pl.BlockSpec((tm, tn), lambda j, i: (i, j)), *side.out_specs],
        out_shape=[jax.ShapeDtypeStruct((T, D), BF16), *side.out_shapes],
        compiler_params=_params("arbitrary", "arbitrary"), name="gate_merge")(
            hb, *ys, w_gate, b_gate, w_branch, *side.arrays)
    return outs[0], side.unflatten(outs[1:])


def _out_ln_kernel(m_ref, w_ref, h_ref, g_ref, b_ref, ho_ref, hbo_ref, hpo_ref, *, alpha):
    tm = m_ref.shape[0]
    mix = jnp.dot(m_ref[...], w_ref[...], preferred_element_type=F32)
    hn = _layer_norm(alpha * h_ref[...] + mix, g_ref[0], b_ref[0])
    ho_ref[...] = hn
    hbo_ref[...] = hn.astype(BF16)
    _store_row_major(hpo_ref, _pack_bf16_pairs(hn), tm)


def _out_ln(merged, w_out, h, g, b, layer, alpha, tm=256):
    T, D = h.shape
    pieces = _row_pitch(D // 2 // LANES)
    row = pl.BlockSpec((tm, D), lambda i: (i, 0))
    vec = pl.BlockSpec((1, 1, D), lambda i: (layer, 0, 0))
    kern = functools.partial(_out_ln_kernel, alpha=alpha)
    return pl.pallas_call(
        kern, grid=(T // tm,),
        in_specs=[row, pl.BlockSpec((D, D), lambda i: (0, 0)), row, vec, vec],
        out_specs=[row, row, pl.BlockSpec((tm * pieces, LANES), lambda i: (i, 0))],
        out_shape=[jax.ShapeDtypeStruct((T, D), F32), jax.ShapeDtypeStruct((T, D), BF16),
                   jax.ShapeDtypeStruct((T * pieces, LANES), U32)],
        compiler_params=_params("parallel"), name="out_ln1")(merged, w_out, h, _rows(g), _rows(b))


def _router_kernel(hb_ref, wr_ref, br_ref, e_ref, gate_ref, rank_ref, cnt_ref, carry_ref):
    tm = hb_ref.shape[0]

    @pl.when(pl.program_id(0) == 0)
    def _():
        carry_ref[...] = jnp.zeros_like(carry_ref)

    nt_dims = (((1,), (1,)), ((), ()))
    logits = lax.dot_general(wr_ref[...], hb_ref[...], nt_dims, preferred_element_type=F32) + br_ref[...]
    eidx = lax.broadcasted_iota(jnp.int32, logits.shape, 0)
    work = logits
    tops, hots = [], []
    for k in range(TOP_K):
        m = jnp.max(work, axis=0, keepdims=True)
        sel = jnp.min(jnp.where(work == m, eidx, N_EXPERTS), axis=0, keepdims=True)
        hot = eidx == sel
        work = jnp.where(hot, -jnp.inf, work)
        e_ref[k:k + 1, :] = sel
        tops.append(m)
        hots.append(hot)
    ex = [jnp.exp(t - tops[0]) for t in tops]
    den = ex[0] + ex[1] + ex[2] + ex[3]
    member = jnp.zeros(logits.shape, F32)
    for hot in hots:
        member = member + jnp.where(hot, 1.0, 0.0)
    tri = jnp.where(lax.broadcasted_iota(jnp.int32, (tm, tm), 0) <= lax.broadcasted_iota(jnp.int32, (tm, tm), 1),
                    1.0, 0.0).astype(BF16)
    incl = jnp.dot(member.astype(BF16), tri, preferred_element_type=F32)
    before = carry_ref[:, 0:1] + incl - member
    for k in range(TOP_K):
        gate_ref[k:k + 1, :] = ex[k] / den
        rank_ref[k:k + 1, :] = jnp.sum(jnp.where(hots[k], before, 0.0), axis=0, keepdims=True).astype(jnp.int32)
    carry_ref[...] = carry_ref[...] + incl[:, tm - 1:tm]
    cnt_ref[...] = carry_ref[...].astype(jnp.int32)


def _router(hb, w_router_t, b_router, tm=512):
    T, D = hb.shape
    E = w_router_t.shape[0]
    tok = pl.BlockSpec((TOP_K, tm), lambda i: (0, i))
    return pl.pallas_call(
        _router_kernel, grid=(T // tm,),
        in_specs=[pl.BlockSpec((tm, D), lambda i: (i, 0)), pl.BlockSpec((E, D), lambda i: (0, 0)),
                  pl.BlockSpec((E, 1), lambda i: (0, 0))],
        out_specs=[tok, tok, tok, pl.BlockSpec((E, 128), lambda i: (0, 0))],
        out_shape=[jax.ShapeDtypeStruct((TOP_K, T), jnp.int32), jax.ShapeDtypeStruct((TOP_K, T), F32),
                   jax.ShapeDtypeStruct((TOP_K, T), jnp.int32), jax.ShapeDtypeStruct((E, 128), jnp.int32)],
        scratch_shapes=[pltpu.VMEM((E, 128), F32)],
        compiler_params=_params("arbitrary"), name="router")(hb, w_router_t, b_router.reshape(E, 1))


def _dispatch_kernel(plo_ref, phi_ref, dst_hbm, hp_ref, xs_hbm, idx_ref, zero_ref, idx_sem, row_sem, zero_sem,
                     *, n_tiles, n_blocks, n):
    assert hp_ref.shape[0] % n == 0 and zero_ref.shape[0] % n == 0
    tm = hp_ref.shape[0] // n
    blk_rows = zero_ref.shape[0]
    bm = blk_rows // n
    i = pl.program_id(0)

    def idx_copy(tile):
        return pltpu.make_async_copy(dst_hbm.at[tile], idx_ref, idx_sem)

    def zero_row(r):
        return pltpu.make_async_copy(zero_ref.at[pl.ds(0, n)], xs_hbm.at[pl.ds(r * n, n)], zero_sem)

    def zero_block(blk):
        return pltpu.make_async_copy(
            zero_ref, xs_hbm.at[pl.ds(pl.multiple_of(blk * blk_rows, 8), blk_rows)], zero_sem)

    def for_each(lo, hi, fn):
        def body(r, _):
            fn(r)
            return 0
        lax.fori_loop(lo, hi, body, 0)

    @pl.when(i == 0)
    def _():
        idx_copy(0).start()
        zero_ref[...] = jnp.zeros_like(zero_ref)
        for e in range(N_EXPERTS):
            for_each(plo_ref[e], phi_ref[e], lambda r: zero_row(r).start())
            for_each(plo_ref[e], phi_ref[e], lambda r: zero_row(r).wait())
        first_unused = phi_ref[N_EXPERTS - 1] // bm
        for_each(first_unused, n_blocks, lambda blk: zero_block(blk).start())
        for_each(first_unused, n_blocks, lambda blk: zero_block(blk).wait())

    idx_copy(i).wait()
    for r in range(tm):
        for k in range(TOP_K):
            d = idx_ref[r * TOP_K + k]
            pltpu.make_async_copy(hp_ref.at[pl.ds(r * n, n)], xs_hbm.at[pl.ds(d, n)], row_sem).start(priority=k % 2)

    @pl.when(i + 1 < n_tiles)
    def _():
        idx_copy(i + 1).start()

    for k in range(TOP_K):
        pltpu.make_async_copy(hp_ref, xs_hbm.at[pl.ds(0, tm * n)], row_sem).wait()


def _dispatch(hp, dest_tiles, pad_lo, pad_hi, n_slots, n):
    n_tiles, ktm = dest_tiles.shape
    tm = ktm // TOP_K
    kern = functools.partial(_dispatch_kernel, n_tiles=n_tiles, n_blocks=n_slots // MOE_BLOCK, n=n)
    grid_spec = pltpu.PrefetchScalarGridSpec(
        num_scalar_prefetch=2, grid=(n_tiles,),
        in_specs=[pl.BlockSpec(memory_space=pl.ANY), pl.BlockSpec((tm * n, LANES), lambda i, lo, hi: (i, 0))],
        out_specs=pl.BlockSpec(memory_space=pl.ANY),
        scratch_shapes=[pltpu.SMEM((ktm,), jnp.int32), pltpu.VMEM((MOE_BLOCK * n, LANES), U32),
                        pltpu.SemaphoreType.DMA, pltpu.SemaphoreType.DMA, pltpu.SemaphoreType.DMA])
    return pl.pallas_call(
        kern, grid_spec=grid_spec, out_shape=jax.ShapeDtypeStruct((n_slots * n, LANES), U32),
        compiler_params=_params("arbitrary"), name="moe_dispatch")(pad_lo, pad_hi, dest_tiles, hp)


def _moe_kernel(be_ref, bn_ref, x_ref, wup_ref, bup_ref, wdn_ref, bdn_ref, y_ref, *, bm):
    ed = wdn_ref.shape[1]
    n_x = wup_ref.shape[1] // 2 // LANES
    i = pl.program_id(0)

    @pl.when(bn_ref[i] > 0)
    def _():
        x = _unpack_bf16_pairs(jnp.concatenate(_load_row_major(x_ref, bm, n_x), axis=1))
        gu = jnp.dot(x, wup_ref[0], preferred_element_type=F32) + bup_ref[0]
        glu = jnp.minimum(gu[:, :ed], SWIGLU_LIMIT)
        lin = jnp.clip(gu[:, ed:], -SWIGLU_LIMIT, SWIGLU_LIMIT)
        act = glu * jax.nn.sigmoid(SWIGLU_ALPHA * glu) * (lin + 1.0)
        y = jnp.dot(act.astype(BF16), wdn_ref[0], preferred_element_type=F32) + bdn_ref[0]
        _store_row_major(y_ref, y, bm)

    @pl.when(bn_ref[i] == 0)
    def _():
        y_ref[...] = jnp.zeros_like(y_ref)


def _moe(xs, blk_e, blk_n, w_up, b_up, w_down, b_down, layer):
    bm = MOE_BLOCK
    _, D, two_ed = w_up.shape
    ed = two_ed // 2
    n_x = _row_pitch(D // 2 // LANES)
    n_y = _row_pitch(D // LANES)
    n_blocks = xs.shape[0] // (bm * n_x)
    base = layer * N_EXPERTS

    def expert(i, be, bn):
        return (be[i], 0, 0)

    def expert_bias(i, be, bn):
        return (base + be[i], 0, 0)

    grid_spec = pltpu.PrefetchScalarGridSpec(
        num_scalar_prefetch=2, grid=(n_blocks,),
        in_specs=[pl.BlockSpec((bm * n_x, LANES), lambda i, be, bn: (jnp.where(bn[i] > 0, i, 0), 0)),
                  pl.BlockSpec((1, D, two_ed), expert), pl.BlockSpec((1, 1, two_ed), expert_bias),
                  pl.BlockSpec((1, ed, D), expert), pl.BlockSpec((1, 1, D), expert_bias)],
        out_specs=pl.BlockSpec((bm * n_y, LANES), lambda i, be, bn: (i, 0)))
    return pl.pallas_call(
        functools.partial(_moe_kernel, bm=bm), grid_spec=grid_spec,
        out_shape=jax.ShapeDtypeStruct((n_blocks * bm * n_y, LANES), F32),
        compiler_params=_params("arbitrary"), name="moe_experts")(blk_e, blk_n, xs, w_up, b_up, w_down, b_down)


def _combine_kernel(dst_hbm, y_hbm, gates_ref, h_ref, g_ref, b_ref, ho_ref, hbo_ref,
                    idx_ref, ybuf, gate_ref, r_ref, idx_sem, row_sem, *, n_tiles, alpha):
    tm, D = h_ref.shape
    n = D // LANES
    pitch = _row_pitch(n)
    per_seg = tm // n
    i = pl.program_id(0)
    slot = i % 2
    prev = 1 - slot

    def idx_copy(tile):
        return pltpu.make_async_copy(dst_hbm.at[tile], idx_ref, idx_sem)

    def issue(seg):
        for r in range(seg * per_seg, (seg + 1) * per_seg):
            for k in range(TOP_K):
                row = idx_ref[r * TOP_K + k]
                pltpu.make_async_copy(y_hbm.at[pl.ds(row, pitch)], ybuf.at[slot, k, pl.ds(r * pitch, pitch)],
                                      row_sem.at[slot]).start(priority=k % 2)

    def wait_rows():
        pltpu.make_async_copy(ybuf.at[prev], ybuf.at[prev], row_sem.at[prev]).wait()
        for k in range(TOP_K):
            gate_ref[k] = jnp.broadcast_to(gates_ref[:, k:k + 1], (tm, LANES))

    def piece(j):
        cols = slice(j * LANES, (j + 1) * LANES)
        acc = alpha * h_ref[:, cols]
        for k in range(TOP_K):
            acc = acc + gate_ref[k] * ybuf[prev, k, pl.ds(j, tm, stride=pitch), :]
        r_ref[:, cols] = acc

    def finish():
        hn = _layer_norm(r_ref[...], g_ref[0], b_ref[0])
        ho_ref[...] = hn
        hbo_ref[...] = hn.astype(BF16)

    @pl.when(i == 0)
    def _():
        idx_copy(0).start()
        idx_copy(0).wait()
        for seg in range(n):
            issue(seg)

    @pl.when(jnp.logical_and(i > 0, i < n_tiles))
    def _():
        idx_copy(i).wait()
        wait_rows()
        for seg in range(n):
            issue(seg)
            piece(seg)

    @pl.when(i + 1 < n_tiles)
    def _():
        idx_copy(i + 1).start()

    @pl.when(i == n_tiles)
    def _():
        wait_rows()
        for j in range(n):
            piece(j)

    @pl.when(i > 0)
    def _():
        finish()


def _combine(dest_tiles, y_sorted, gates_t, h, g, b, layer, alpha):
    T, D = h.shape
    n_tiles, ktm = dest_tiles.shape
    tm = ktm // TOP_K

    def tile(i):
        return (jnp.maximum(i - 1, 0), 0)

    row = pl.BlockSpec((tm, D), tile)
    vec = pl.BlockSpec((1, 1, D), lambda i: (layer, 0, 0))
    kern = functools.partial(_combine_kernel, n_tiles=n_tiles, alpha=alpha)
    return pl.pallas_call(
        kern, grid=(n_tiles + 1,),
        in_specs=[pl.BlockSpec(memory_space=pl.ANY), pl.BlockSpec(memory_space=pl.ANY),
                  pl.BlockSpec((tm, TOP_K), tile), row, vec, vec],
        out_specs=[row, row],
        out_shape=[jax.ShapeDtypeStruct((T, D), F32), jax.ShapeDtypeStruct((T, D), BF16)],
        scratch_shapes=[pltpu.SMEM((ktm,), jnp.int32),
                        pltpu.VMEM((2, TOP_K, tm * _row_pitch(D // LANES), LANES), F32),
                        pltpu.VMEM((TOP_K, tm, LANES), F32), pltpu.VMEM((tm, D), F32),
                        pltpu.SemaphoreType.DMA, pltpu.SemaphoreType.DMA((2,))],
        compiler_params=_params("arbitrary"), name="moe_combine_ln2")(
            dest_tiles, y_sorted, gates_t, h, _rows(g), _rows(b))


def _routing_plan(top_e, rank, counts, n_blocks):
    T = top_e.shape[1]
    bm = MOE_BLOCK
    experts = jnp.arange(N_EXPERTS, dtype=jnp.int32)
    padded = (counts + bm - 1) // bm * bm
    pad_end = jnp.cumsum(padded)
    pad_start = pad_end - padded
    start_of = jnp.sum(jnp.where(top_e[None] == experts[:, None, None], pad_start[:, None, None], 0), axis=0)
    dest = start_of + rank
    blk_start = jnp.arange(n_blocks, dtype=jnp.int32) * bm
    blk_e = jnp.minimum(jnp.sum(pad_end[None, :] <= blk_start[:, None], axis=1), N_EXPERTS - 1).astype(jnp.int32)
    hot = blk_e[:, None] == experts[None, :]
    blk_cnt = jnp.sum(jnp.where(hot, counts[None, :], 0), axis=1)
    blk_pad_start = jnp.sum(jnp.where(hot, pad_start[None, :], 0), axis=1)
    blk_n = jnp.clip(blk_cnt - (blk_start - blk_pad_start), 0, bm).astype(jnp.int32)
    tm = ROUTE_TILE
    dest_tiles = dest.T.reshape(T // tm, tm * TOP_K).astype(jnp.int32)
    return dest_tiles, blk_e, blk_n, (pad_start + counts).astype(jnp.int32), pad_end.astype(jnp.int32)


def kernel(x, mem, ln_in_g, ln_in_b, w_in, pool_w, pool_scale, ret_log_gamma, ret_gn_g, sgu_ln_g, sgu_ln_b, sgu_w, sgu_b, w_mem_kv, w_branch, w_gate, b_gate, w_out, ln1_g, ln1_b, w_router, b_router, w_up, b_up, w_down, b_down, ln2_g, ln2_b):
    B, S, D = x.shape
    depth = w_in.shape[0]
    n_mem = mem.shape[1]
    T = B * S
    alpha = (2 * depth) ** 0.25
    head_dim = ret_gn_g.shape[1] // N_RET_HEADS
    n_blocks = -(-(T * TOP_K + N_EXPERTS * (MOE_BLOCK - 1)) // MOE_BLOCK)

    pos = jnp.arange(S, dtype=F32)
    inv_freq = jnp.exp(-math.log(ROPE_BASE) * jnp.arange(0, head_dim, 2, dtype=F32) / head_dim)
    ang = pos[:, None] * inv_freq[None, :]
    cos, sin = jnp.cos(ang), jnp.sin(ang)
    cos2 = jnp.concatenate([cos, cos], axis=1)
    sin2 = jnp.concatenate([-sin, sin], axis=1)

    pool_wb, sgu_wb = pool_w.astype(BF16), sgu_w.astype(BF16)
    w_router_t = jnp.swapaxes(w_router, 1, 2).astype(BF16)
    E = w_up.shape[1]
    b_up_r = b_up.reshape(depth * E, 1, -1)
    b_down_r = b_down.reshape(depth * E, 1, D)
    pitch_x = _row_pitch(D // 2 // LANES)
    pitch_y = _row_pitch(D // LANES)

    mem_b = mem.reshape(B * n_mem, D).astype(BF16)
    h, hb = _ln_in(x.reshape(T, D), ln_in_g, ln_in_b)
    for l in range(depth):
        z, (w_gate_b, w_branch_b, w_out_b) = _matmul(hb, w_in, l, tm=1024, tn=1024, name="in_proj",
                                                     side_weights=(w_gate, w_branch, w_out))
        kv, _ = _matmul(mem_b, w_mem_kv, l, tm=B * n_mem, tn=1024, name="mem_kv")
        ys = (
            _pool(z, pool_wb, pool_scale, l, S),
            _retention(z, ret_log_gamma, ret_gn_g, cos2, sin2, l, B, S),
            _sgu(z, sgu_ln_g, sgu_ln_b, sgu_wb, sgu_b, l),
            _mem_attn(z, kv, S, n_mem),
        )
        merged, (w_up_b, w_down_b) = _gate_merge(hb, ys, w_gate_b, b_gate, w_branch_b, l,
                                                 side_weights=(w_up, w_down))
        h, hb, hp = _out_ln(merged, w_out_b, h, ln1_g, ln1_b, l, alpha)

        top_e, gates, rank, cnt = _router(hb, w_router_t[l], b_router[l])
        dest_tiles, blk_e, blk_n, pad_lo, pad_hi = _routing_plan(top_e, rank, cnt[:, 0], n_blocks)
        xs = _dispatch(hp, dest_tiles * pitch_x, pad_lo, pad_hi, n_blocks * MOE_BLOCK, pitch_x)
        y_sorted = _moe(xs, blk_e, blk_n, w_up_b, b_up_r, w_down_b, b_down_r, l)
        h, hb = _combine(dest_tiles * pitch_y, y_sorted, gates.T, h, ln2_g, ln2_b, l, alpha)
    return h.reshape(B, S, D)
```

```python
import functools
import math

import jax
import jax.numpy as jnp
from jax import lax
from jax.experimental import pallas as pl
from jax.experimental.pallas import tpu as pltpu

F32 = jnp.float32
BF16 = jnp.bfloat16
U32 = jnp.uint32

N_BRANCHES = 4
POOL_WINDOWS = (2, 4, 8, 16)
N_RET_HEADS = 8
RET_CHUNK = 128
ROPE_BASE = 10000.0
SGU_CHUNK = 128
N_SGU_GROUPS = 4
N_MEM_HEADS = 4
N_EXPERTS = 32
TOP_K = 4
SWIGLU_LIMIT = 7.0
SWIGLU_ALPHA = 1.702
LN_EPS = 1e-5

V7X_VMEM_BYTES = 64 * 1024 * 1024
VMEM_LIMIT_BYTES = V7X_VMEM_BYTES - 8 * 1024 * 1024
BF16_SUBLANE_TILE = 16
LANES = 128

MOE_BLOCK = 256
ROUTE_TILE = 128

def _params(*semantics):
    return pltpu.CompilerParams(dimension_semantics=semantics, vmem_limit_bytes=VMEM_LIMIT_BYTES)


def _rows(v):
    return v.reshape(v.shape[0], 1, v.shape[1])


def _layer_norm(x, g, b):
    mu = jnp.mean(x, axis=-1, keepdims=True)
    xc = x - mu
    var = jnp.mean(xc * xc, axis=-1, keepdims=True)
    return xc * lax.rsqrt(var + LN_EPS) * g + b


def _pack_bf16_pairs(x):
    n = x.shape[1] // 2
    xb = x.astype(BF16).astype(F32)
    lo = lax.bitcast_convert_type(xb[:, :n], U32)
    hi = lax.bitcast_convert_type(xb[:, n:], U32)
    return (lo >> 16) | hi


def _row_pitch(n):
    assert n % 2 == 0
    return n + 1


def _store_row_major(ref, x, rows):
    n = x.shape[1] // LANES
    pitch = _row_pitch(n)
    for j in range(n):
        ref[pl.ds(j, rows, stride=pitch), :] = x[:, j * LANES:(j + 1) * LANES]
    ref[pl.ds(n, rows, stride=pitch), :] = jnp.zeros((rows, LANES), ref.dtype)


def _load_row_major(ref, rows, n, index=()):
    return [ref[(*index, pl.ds(j, rows, stride=_row_pitch(n)), slice(None))] for j in range(n)]


def _unpack_bf16_pairs(p):
    lo = lax.bitcast_convert_type(p << 16, F32).astype(BF16)
    hi = lax.bitcast_convert_type(p & jnp.uint32(0xFFFF0000), F32).astype(BF16)
    return jnp.concatenate([lo, hi], axis=1)


def _ln_in_kernel(x_ref, g_ref, b_ref, h_ref, hb_ref):
    h = _layer_norm(x_ref[...], g_ref[...], b_ref[...])
    h_ref[...] = h
    hb_ref[...] = h.astype(BF16)


def _ln_in(x, g, b, tm=512):
    T, D = x.shape
    row = pl.BlockSpec((tm, D), lambda i: (i, 0))
    vec = pl.BlockSpec((1, D), lambda i: (0, 0))
    return pl.pallas_call(
        _ln_in_kernel, grid=(T // tm,), in_specs=[row, vec, vec], out_specs=[row, row],
        out_shape=[jax.ShapeDtypeStruct((T, D), F32), jax.ShapeDtypeStruct((T, D), BF16)],
        compiler_params=_params("parallel"), name="ln_in")(x, g.reshape(1, D), b.reshape(1, D))


class _SideCasts:
    def __init__(self, weights, layer, n_steps, step_of):
        self.arrays, self.in_specs, self.out_specs, self.out_shapes, self.shapes = [], [], [], [], []
        for w in weights:
            cols = w.shape[-1]
            rows = math.prod(w.shape[1:-1])
            slab = rows // n_steps
            assert slab * n_steps == rows and slab % BF16_SUBLANE_TILE == 0, (w.shape, n_steps)
            self.arrays.append(w.reshape(w.shape[0] * rows, cols))
            self.in_specs.append(pl.BlockSpec((slab, cols), lambda *g: (layer * n_steps + step_of(*g), 0)))
            self.out_specs.append(pl.BlockSpec((slab, cols), lambda *g: (step_of(*g), 0)))
            self.out_shapes.append(jax.ShapeDtypeStruct((rows, cols), BF16))
            self.shapes.append(w.shape[1:])

    def __len__(self):
        return len(self.arrays)

    @staticmethod
    def run(in_refs, out_refs):
        for i_ref, o_ref in zip(in_refs, out_refs):
            o_ref[...] = i_ref[...].astype(BF16)

    def unflatten(self, outs):
        return [o.reshape(s) for o, s in zip(outs, self.shapes)]


def _matmul_kernel(a_ref, w_ref, *refs, n_side):
    side_in, o_ref, side_out, wb_ref = refs[:n_side], refs[n_side], refs[n_side + 1:-1], refs[-1]

    @pl.when(pl.program_id(1) == 0)
    def _():
        wb_ref[...] = w_ref[0].astype(BF16)

    o_ref[...] = jnp.dot(a_ref[...], wb_ref[...], preferred_element_type=F32).astype(o_ref.dtype)
    _SideCasts.run(side_in, side_out)


def _matmul(a, w, layer, tm, tn, name, side_weights=()):
    M, K = a.shape
    N = w.shape[2]
    n_i = M // tm
    side = _SideCasts(side_weights, layer, (N // tn) * n_i, lambda j, i: j * n_i + i)
    outs = pl.pallas_call(
        functools.partial(_matmul_kernel, n_side=len(side)), grid=(N // tn, n_i),
        in_specs=[pl.BlockSpec((tm, K), lambda j, i: (i, 0)), pl.BlockSpec((1, K, tn), lambda j, i: (layer, 0, j)),
                  *side.in_specs],
        out_specs=[pl.BlockSpec((tm, tn), lambda j, i: (i, j)), *side.out_specs],
        out_shape=[jax.ShapeDtypeStruct((M, N), BF16), *side.out_shapes],
        scratch_shapes=[pltpu.VMEM((K, tn), BF16)],
        compiler_params=_params("arbitrary", "arbitrary"), name=name)(a, w, *side.arrays)
    return outs[0], side.unflatten(outs[1:])


POOL_HALO = 64


def _pool_bands(ts):
    r = jnp.arange(ts)[:, None] + POOL_HALO
    c = jnp.arange(ts + 2 * POOL_HALO)[None, :]
    return jnp.stack([((c >= r - w // 2) & (c < r + w // 2)) for w in POOL_WINDOWS]).astype(BF16)


def _pool_kernel(z_ref, prev_ref, next_ref, band_ref, w_ref, scale_ref, o_ref, ext_ref, *, seq, ts):
    halo = POOL_HALO
    gd = w_ref.shape[2]
    pos0 = (pl.program_id(0) % (seq // ts)) * ts
    ext_ref[0:halo, :] = jnp.where(pos0 == 0, jnp.zeros_like(prev_ref), prev_ref[...])
    ext_ref[halo:halo + ts, :] = z_ref[...]
    ext_ref[halo + ts:, :] = jnp.where(pos0 + ts == seq, jnp.zeros_like(next_ref), next_ref[...])
    pos = pos0 + lax.broadcasted_iota(jnp.int32, (ts, gd), 0)
    for g, w in enumerate(POOL_WINDOWS):
        cols = slice(g * gd, (g + 1) * gd)
        half = w // 2
        acc = jnp.dot(band_ref[g], ext_ref[:, cols], preferred_element_type=F32)
        cnt = jnp.minimum(pos + half, seq) - jnp.maximum(pos - half, 0)
        mixed = acc / cnt.astype(F32) - z_ref[:, cols].astype(F32)
        y = jnp.dot(mixed.astype(BF16), w_ref[0, g], preferred_element_type=F32)
        o_ref[:, cols] = (y * scale_ref[0, :, cols]).astype(o_ref.dtype)


def _retention_kernel(lg_ref, q_ref, k_ref, v_ref, g_ref, cos_ref, sin_ref, gn_ref, o_ref,
                      rs_ref, kr_ref, kvf_ref, *, seq, unroll_bwd, unroll_fwd):
    C = RET_CHUNK
    d = q_ref.shape[1]
    nc = seq // C
    head = pl.program_id(0) % N_RET_HEADS
    lgf = lg_ref[0, head]
    lgb = lg_ref[1, head]
    row = lax.broadcasted_iota(jnp.int32, (C, C), 0).astype(F32)
    col = lax.broadcasted_iota(jnp.int32, (C, C), 1).astype(F32)
    diff = row - col
    decay = jnp.where(diff >= 0, jnp.exp(lgf * jnp.maximum(diff, 0.0)), jnp.exp(lgb * jnp.maximum(-diff, 0.0)))
    rowd = lax.broadcasted_iota(jnp.int32, (C, d), 0).astype(F32)
    qwf = jnp.exp(lgf * (rowd + 1.0))
    qwb = jnp.exp(lgb * (C - rowd))
    kwf = jnp.exp(lgf * (C - 1.0 - rowd))
    kwb = jnp.exp(lgb * rowd)
    chunk_decay_f = jnp.exp(lgf * jnp.full((d, d), float(C), F32))
    chunk_decay_b = jnp.exp(lgb * jnp.full((d, d), float(C), F32))
    tn_dims = (((0,), (0,)), ((), ()))
    nt_dims = (((1,), (1,)), ((), ()))

    def rotary(ref, sl):
        x = ref[sl, :].astype(F32)
        return x * cos_ref[sl, :] + pltpu.roll(x, d // 2, axis=1) * sin_ref[sl, :]

    def chunk(n):
        return pl.ds(pl.multiple_of(n * C, C), C)

    def bwd_body(j, state):
        n = nc - 1 - j
        sl = chunk(n)
        rs_ref[n] = state.astype(BF16)
        k = rotary(k_ref, sl) * (d ** -0.5)
        kr_ref[sl, :] = k.astype(BF16)
        kw = jnp.concatenate([k * kwf, k * kwb], axis=1).astype(BF16)
        kv = lax.dot_general(kw, v_ref[sl, :], tn_dims, preferred_element_type=F32)
        kvf_ref[n] = kv[:d]
        return state * chunk_decay_b + kv[d:]

    lax.fori_loop(0, nc, bwd_body, jnp.zeros((d, d), F32), unroll=unroll_bwd)

    def fwd_body(n, state):
        sl = chunk(n)
        qb = rotary(q_ref, sl).astype(BF16)
        v = v_ref[sl, :]
        scores = lax.dot_general(qb, kr_ref[sl, :], nt_dims, preferred_element_type=F32) * decay
        y = jnp.dot(scores.astype(BF16), v, preferred_element_type=F32)
        states = jnp.concatenate([state.astype(BF16), rs_ref[n]], axis=1)
        cross = jnp.dot(qb, states, preferred_element_type=F32)
        y = y + cross[:, :d] * qwf + cross[:, d:] * qwb
        mu = jnp.mean(y, axis=-1, keepdims=True)
        yc = y - mu
        var = jnp.mean(yc * yc, axis=-1, keepdims=True)
        yn = yc * lax.rsqrt(var + LN_EPS) * gn_ref[0]
        gate = g_ref[sl, :].astype(F32)
        o_ref[sl, :] = (yn * (gate * jax.nn.sigmoid(gate))).astype(o_ref.dtype)
        return state * chunk_decay_f + kvf_ref[n]

    lax.fori_loop(0, nc, fwd_body, jnp.zeros((d, d), F32), unroll=unroll_fwd)


def _retention(z, log_gamma, gn_g, cos2, sin2, layer, batch, seq, unroll_bwd=8, unroll_fwd=16):
    T = z.shape[0]
    H = N_RET_HEADS
    d = gn_g.shape[1] // H
    nc = seq // RET_CHUNK

    def col(split):
        return pl.BlockSpec((seq, d), lambda i: (i // H, split * H + i % H))

    table = pl.BlockSpec((seq, d), lambda i: (0, 0))
    kern = functools.partial(_retention_kernel, seq=seq, unroll_bwd=unroll_bwd, unroll_fwd=unroll_fwd)
    return pl.pallas_call(
        kern, grid=(batch * H,),
        in_specs=[pl.BlockSpec(memory_space=pltpu.SMEM), col(1), col(2), col(3), col(4), table, table,
                  pl.BlockSpec((1, 1, d), lambda i: (layer, 0, i % H))],
        out_specs=pl.BlockSpec((seq, d), lambda i: (i // H, i % H)),
        out_shape=jax.ShapeDtypeStruct((T, H * d), BF16),
        scratch_shapes=[pltpu.VMEM((nc, d, d), BF16), pltpu.VMEM((seq, d), BF16), pltpu.VMEM((nc, d, d), F32)],
        compiler_params=_params("parallel"), name="retention")(
            log_gamma[layer], z, z, z, z, cos2, sin2, _rows(gn_g))


def _sgu_kernel(u_ref, v_ref, lng_ref, lnb_ref, w_ref, bias_ref, o_ref, *, chunks):
    P = SGU_CHUNK
    gd = o_ref.shape[1] // N_SGU_GROUPS

    def body(c, _):
        rows = pl.ds(pl.multiple_of(c * P, P), P)
        vn = _layer_norm(jax.nn.gelu(v_ref[rows, :].astype(F32)), lng_ref[0], lnb_ref[0]).astype(BF16)
        for g in range(N_SGU_GROUPS):
            cols = slice(g * gd, (g + 1) * gd)
            mixed = jnp.dot(w_ref[0, g], vn[:, cols], preferred_element_type=F32) + bias_ref[:, cols]
            u = jax.nn.gelu(u_ref[rows, cols].astype(F32))
            o_ref[rows, cols] = (u * mixed).astype(o_ref.dtype)
        return 0

    lax.fori_loop(0, chunks, body, 0, unroll=True)


def _mem_attn_kernel(q_ref, kv_ref, o_ref):
    width = o_ref.shape[1]
    hd = width // N_MEM_HEADS
    nt_dims = (((1,), (1,)), ((), ()))
    for h in range(N_MEM_HEADS):
        cols = slice(h * hd, (h + 1) * hd)
        s = lax.dot_general(q_ref[:, cols], kv_ref[:, cols], nt_dims, preferred_element_type=F32) * (hd ** -0.5)
        p = jnp.exp(s - jnp.max(s, axis=-1, keepdims=True))
        p = p / jnp.sum(p, axis=-1, keepdims=True)
        v = kv_ref[:, width + h * hd:width + (h + 1) * hd]
        o_ref[:, cols] = jnp.dot(p.astype(BF16), v, preferred_element_type=F32).astype(o_ref.dtype)


def _local_branches_kernel(a_ref, prev_ref, next_ref, band_ref, pw_ref, ps_ref, u_ref, v_ref, lng_ref, lnb_ref,
                           sw_ref, bias_ref, q_ref, kv_ref, pool_ref, sgu_ref, mem_ref, ext_ref, *, seq, ts):
    _pool_kernel(a_ref, prev_ref, next_ref, band_ref, pw_ref, ps_ref, pool_ref, ext_ref, seq=seq, ts=ts)
    _sgu_kernel(u_ref, v_ref, lng_ref, lnb_ref, sw_ref, bias_ref, sgu_ref, chunks=ts // SGU_CHUNK)
    _mem_attn_kernel(q_ref, kv_ref, mem_ref)


def _local_branches(z, kv, pool_w, pool_scale, sgu_ln_g, sgu_ln_b, sgu_w, sgu_b, layer, seq, n_mem, ts=512):
    T = z.shape[0]
    _, G, gd, _ = pool_w.shape
    width = G * gd
    _, SG, P, _ = sgu_w.shape
    halo = POOL_HALO
    n_halo = T // halo
    bias = jnp.repeat(sgu_b[layer].T.astype(F32), width // SG, axis=1)
    kern = functools.partial(_local_branches_kernel, seq=seq, ts=ts)

    def rows(split):
        return pl.BlockSpec((ts, width), lambda i: (i, split))

    vec = pl.BlockSpec((1, 1, width), lambda i: (layer, 0, 0))
    out = jax.ShapeDtypeStruct((T, width), BF16)
    return pl.pallas_call(
        kern, grid=(T // ts,),
        in_specs=[
            rows(0),
            pl.BlockSpec((halo, width), lambda i: (jnp.maximum(i * (ts // halo) - 1, 0), 0)),
            pl.BlockSpec((halo, width), lambda i: (jnp.minimum((i + 1) * (ts // halo), n_halo - 1), 0)),
            pl.BlockSpec((G, ts, ts + 2 * halo), lambda i: (0, 0, 0)),
            pl.BlockSpec((1, G, gd, gd), lambda i: (layer, 0, 0, 0)), vec,
            rows(5), rows(6), vec, vec, pl.BlockSpec((1, SG, P, P), lambda i: (layer, 0, 0, 0)),
            pl.BlockSpec((P, width), lambda i: (0, 0)),
            rows(7), pl.BlockSpec((n_mem, 2 * width), lambda i: (i // (seq // ts), 0)),
        ],
        out_specs=[rows(0), rows(0), rows(0)],
        out_shape=[out, out, out],
        scratch_shapes=[pltpu.VMEM((ts + 2 * halo, width), BF16)],
        compiler_params=_params("parallel"), name="local_branches")(
            z, z, z, _pool_bands(ts), pool_w, _rows(pool_scale), z, z, _rows(sgu_ln_g), _rows(sgu_ln_b), sgu_w, bias,
            z, kv)


def _gate_merge_kernel(hb_ref, y0_ref, y1_ref, y2_ref, y3_ref, wg_ref, bg_ref, wb_ref, *refs, n_side):
    side_in, o_ref, side_out = refs[:n_side], refs[n_side], refs[n_side + 1:]
    acc = None
    for b, y_ref in enumerate((y0_ref, y1_ref, y2_ref, y3_ref)):
        gate = jax.nn.sigmoid(jnp.dot(hb_ref[...], wg_ref[b], preferred_element_type=F32) + bg_ref[0, b:b + 1, :])
        term = gate * jnp.dot(y_ref[...], wb_ref[b], preferred_element_type=F32)
        acc = term if acc is None else acc + term
    o_ref[...] = acc.astype(o_ref.dtype)
    _SideCasts.run(side_in, side_out)


def _gate_merge(hb, ys, w_gate, b_gate, w_branch, layer, side_weights=(), tm=512, tn=512):
    T, D = hb.shape
    bw = w_branch.shape[1]
    n_i = T // tm
    side = _SideCasts(side_weights, layer, (D // tn) * n_i, lambda j, i: j * n_i + i)
    yspec = pl.BlockSpec((tm, bw), lambda j, i: (i, 0))
    once = pl.Buffered(1)
    outs = pl.pallas_call(
        functools.partial(_gate_merge_kernel, n_side=len(side)), grid=(D // tn, n_i),
        in_specs=[pl.BlockSpec((tm, D), lambda j, i: (i, 0)), yspec, yspec, yspec, yspec,
                  pl.BlockSpec((N_BRANCHES, D, tn), lambda j, i: (0, 0, j), pipeline_mode=once),
                  pl.BlockSpec((1, N_BRANCHES, tn), lambda j, i: (layer, 0, j)),
                  pl.BlockSpec((N_BRANCHES, bw, tn), lambda j, i: (0, 0, j), pipeline_mode=once),
                  *side.in_specs],
        out_specs=[pl.BlockSpec((tm, tn), lambda j, i: (i, j)), *side.out_specs],
        out_shape=[jax.ShapeDtypeStruct((T, D), BF16), *side.out_shapes],
        compiler_params=_params("arbitrary", "arbitrary"), name="gate_merge")(
            hb, *ys, w_gate, b_gate, w_branch, *side.arrays)
    return outs[0], side.unflatten(outs[1:])


def _route(hb, wr_ref, br_ref, e_ref, gate_ref, rank_ref, cnt_ref, carry_ref):
    tm = hb.shape[0]

    @pl.when(pl.program_id(0) == 0)
    def _():
        carry_ref[...] = jnp.zeros_like(carry_ref)

    nt_dims = (((1,), (1,)), ((), ()))
    logits = lax.dot_general(wr_ref[0], hb, nt_dims, preferred_element_type=F32) + br_ref[0]
    eidx = lax.broadcasted_iota(jnp.int32, logits.shape, 0)
    work = logits
    tops, hots = [], []
    for k in range(TOP_K):
        m = jnp.max(work, axis=0, keepdims=True)
        sel = jnp.min(jnp.where(work == m, eidx, N_EXPERTS), axis=0, keepdims=True)
        hot = eidx == sel
        work = jnp.where(hot, -jnp.inf, work)
        e_ref[k:k + 1, :] = sel
        tops.append(m)
        hots.append(hot)
    ex = [jnp.exp(t - tops[0]) for t in tops]
    den = ex[0] + ex[1] + ex[2] + ex[3]
    member = jnp.zeros(logits.shape, F32)
    for hot in hots:
        member = member + jnp.where(hot, 1.0, 0.0)
    tri = jnp.where(lax.broadcasted_iota(jnp.int32, (tm, tm), 0) <= lax.broadcasted_iota(jnp.int32, (tm, tm), 1),
                    1.0, 0.0).astype(BF16)
    incl = jnp.dot(member.astype(BF16), tri, preferred_element_type=F32)
    before = carry_ref[:, 0:1] + incl - member
    for k in range(TOP_K):
        gate_ref[k:k + 1, :] = ex[k] / den
        rank_ref[k:k + 1, :] = jnp.sum(jnp.where(hots[k], before, 0.0), axis=0, keepdims=True).astype(jnp.int32)
    carry_ref[...] = carry_ref[...] + incl[:, tm - 1:tm]
    cnt_ref[...] = carry_ref[...].astype(jnp.int32)


def _out_ln_kernel(m_ref, w_ref, h_ref, g_ref, b_ref, wr_ref, br_ref,
                   ho_ref, hpo_ref, e_ref, gate_ref, rank_ref, cnt_ref, carry_ref, *, alpha):
    tm = m_ref.shape[0]
    mix = jnp.dot(m_ref[...], w_ref[...], preferred_element_type=F32)
    hn = _layer_norm(alpha * h_ref[...] + mix, g_ref[0], b_ref[0])
    ho_ref[...] = hn
    _store_row_major(hpo_ref, _pack_bf16_pairs(hn), tm)
    _route(hn.astype(BF16), wr_ref, br_ref, e_ref, gate_ref, rank_ref, cnt_ref, carry_ref)


def _out_ln(merged, w_out, h, g, b, w_router_t, b_router, layer, alpha, tm=256):
    T, D = h.shape
    E = w_router_t.shape[1]
    pieces = _row_pitch(D // 2 // LANES)
    row = pl.BlockSpec((tm, D), lambda i: (i, 0))
    vec = pl.BlockSpec((1, 1, D), lambda i: (layer, 0, 0))
    tok = pl.BlockSpec((TOP_K, tm), lambda i: (0, i))
    kern = functools.partial(_out_ln_kernel, alpha=alpha)
    return pl.pallas_call(
        kern, grid=(T // tm,),
        in_specs=[row, pl.BlockSpec((D, D), lambda i: (0, 0)), row, vec, vec,
                  pl.BlockSpec((1, E, D), lambda i: (layer, 0, 0)), pl.BlockSpec((1, E, 1), lambda i: (layer, 0, 0))],
        out_specs=[row, pl.BlockSpec((tm * pieces, LANES), lambda i: (i, 0)), tok, tok, tok,
                   pl.BlockSpec((E, LANES), lambda i: (0, 0))],
        out_shape=[jax.ShapeDtypeStruct((T, D), F32), jax.ShapeDtypeStruct((T * pieces, LANES), U32),
                   jax.ShapeDtypeStruct((TOP_K, T), jnp.int32), jax.ShapeDtypeStruct((TOP_K, T), F32),
                   jax.ShapeDtypeStruct((TOP_K, T), jnp.int32), jax.ShapeDtypeStruct((E, LANES), jnp.int32)],
        scratch_shapes=[pltpu.VMEM((E, LANES), F32)],
        compiler_params=_params("arbitrary"), name="out_ln1_router")(
            merged, w_out, h, _rows(g), _rows(b), w_router_t, b_router.reshape(b_router.shape[0], E, 1))


def _dispatch_kernel(plo_ref, phi_ref, dst_hbm, hp_ref, xs_hbm, idx_ref, zero_ref, idx_sem, row_sem, zero_sem,
                     *, n_tiles, n_blocks, n):
    assert hp_ref.shape[0] % n == 0 and zero_ref.shape[0] % n == 0
    tm = hp_ref.shape[0] // n
    blk_rows = zero_ref.shape[0]
    bm = blk_rows // n
    i = pl.program_id(0)

    def idx_copy(tile):
        return pltpu.make_async_copy(dst_hbm.at[tile], idx_ref, idx_sem)

    def zero_row(r):
        return pltpu.make_async_copy(zero_ref.at[pl.ds(0, n)], xs_hbm.at[pl.ds(r * n, n)], zero_sem)

    def zero_block(blk):
        return pltpu.make_async_copy(
            zero_ref, xs_hbm.at[pl.ds(pl.multiple_of(blk * blk_rows, 8), blk_rows)], zero_sem)

    def for_each(lo, hi, fn):
        def body(r, _):
            fn(r)
            return 0
        lax.fori_loop(lo, hi, body, 0)

    @pl.when(i == 0)
    def _():
        idx_copy(0).start()
        zero_ref[...] = jnp.zeros_like(zero_ref)
        for e in range(N_EXPERTS):
            for_each(plo_ref[e], phi_ref[e], lambda r: zero_row(r).start())
            for_each(plo_ref[e], phi_ref[e], lambda r: zero_row(r).wait())
        first_unused = phi_ref[N_EXPERTS - 1] // bm
        for_each(first_unused, n_blocks, lambda blk: zero_block(blk).start())
        for_each(first_unused, n_blocks, lambda blk: zero_block(blk).wait())

    idx_copy(i).wait()
    for r in range(tm):
        for k in range(TOP_K):
            d = idx_ref[r * TOP_K + k]
            pltpu.make_async_copy(hp_ref.at[pl.ds(r * n, n)], xs_hbm.at[pl.ds(d, n)], row_sem).start(priority=k % 2)

    @pl.when(i + 1 < n_tiles)
    def _():
        idx_copy(i + 1).start()

    for k in range(TOP_K):
        pltpu.make_async_copy(hp_ref, xs_hbm.at[pl.ds(0, tm * n)], row_sem).wait()


def _dispatch(hp, dest_tiles, pad_lo, pad_hi, n_slots, n):
    n_tiles, ktm = dest_tiles.shape
    tm = ktm // TOP_K
    kern = functools.partial(_dispatch_kernel, n_tiles=n_tiles, n_blocks=n_slots // MOE_BLOCK, n=n)
    grid_spec = pltpu.PrefetchScalarGridSpec(
        num_scalar_prefetch=2, grid=(n_tiles,),
        in_specs=[pl.BlockSpec(memory_space=pl.ANY), pl.BlockSpec((tm * n, LANES), lambda i, lo, hi: (i, 0))],
        out_specs=pl.BlockSpec(memory_space=pl.ANY),
        scratch_shapes=[pltpu.SMEM((ktm,), jnp.int32), pltpu.VMEM((MOE_BLOCK * n, LANES), U32),
                        pltpu.SemaphoreType.DMA, pltpu.SemaphoreType.DMA, pltpu.SemaphoreType.DMA])
    return pl.pallas_call(
        kern, grid_spec=grid_spec, out_shape=jax.ShapeDtypeStruct((n_slots * n, LANES), U32),
        compiler_params=_params("arbitrary"), name="moe_dispatch")(pad_lo, pad_hi, dest_tiles, hp)


def _moe_kernel(be_ref, bn_ref, x_ref, wup_ref, bup_ref, wdn_ref, bdn_ref, y_ref, *, bm):
    ed = wdn_ref.shape[1]
    n_x = wup_ref.shape[1] // 2 // LANES
    i = pl.program_id(0)

    def ffn(rows):
        x = _unpack_bf16_pairs(jnp.concatenate(_load_row_major(x_ref, rows, n_x), axis=1))
        gu = jnp.dot(x, wup_ref[0], preferred_element_type=F32) + bup_ref[0]
        glu = jnp.minimum(gu[:, :ed], SWIGLU_LIMIT)
        lin = jnp.clip(gu[:, ed:], -SWIGLU_LIMIT, SWIGLU_LIMIT)
        act = glu * jax.nn.sigmoid(SWIGLU_ALPHA * glu) * (lin + 1.0)
        y = jnp.dot(act.astype(BF16), wdn_ref[0], preferred_element_type=F32) + bdn_ref[0]
        _store_row_major(y_ref, y, rows)
        if rows < bm:
            rest = y_ref.shape[0] // bm * rows
            y_ref[rest:, :] = jnp.zeros((y_ref.shape[0] - rest, LANES), y_ref.dtype)

    @pl.when(bn_ref[i] > bm // 2)
    def _():
        ffn(bm)

    @pl.when(jnp.logical_and(bn_ref[i] > 0, bn_ref[i] <= bm // 2))
    def _():
        ffn(bm // 2)

    @pl.when(bn_ref[i] == 0)
    def _():
        y_ref[...] = jnp.zeros_like(y_ref)


def _moe(xs, blk_e, blk_n, w_up, b_up, w_down, b_down, layer):
    bm = MOE_BLOCK
    _, D, two_ed = w_up.shape
    ed = two_ed // 2
    n_x = _row_pitch(D // 2 // LANES)
    n_y = _row_pitch(D // LANES)
    n_blocks = xs.shape[0] // (bm * n_x)
    base = layer * N_EXPERTS

    def expert(i, be, bn):
        return (be[i], 0, 0)

    def expert_bias(i, be, bn):
        return (base + be[i], 0, 0)

    grid_spec = pltpu.PrefetchScalarGridSpec(
        num_scalar_prefetch=2, grid=(n_blocks,),
        in_specs=[pl.BlockSpec((bm * n_x, LANES), lambda i, be, bn: (jnp.where(bn[i] > 0, i, 0), 0)),
                  pl.BlockSpec((1, D, two_ed), expert), pl.BlockSpec((1, 1, two_ed), expert_bias),
                  pl.BlockSpec((1, ed, D), expert), pl.BlockSpec((1, 1, D), expert_bias)],
        out_specs=pl.BlockSpec((bm * n_y, LANES), lambda i, be, bn: (i, 0)))
    return pl.pallas_call(
        functools.partial(_moe_kernel, bm=bm), grid_spec=grid_spec,
        out_shape=jax.ShapeDtypeStruct((n_blocks * bm * n_y, LANES), F32),
        compiler_params=_params("arbitrary"), name="moe_experts")(blk_e, blk_n, xs, w_up, b_up, w_down, b_down)


def _combine_kernel(dst_hbm, y_hbm, gates_ref, h_ref, g_ref, b_ref, ho_ref, hbo_ref,
                    idx_ref, ybuf, gate_ref, r_ref, idx_sem, row_sem, *, n_tiles, alpha):
    tm, D = h_ref.shape
    n = D // LANES
    pitch = _row_pitch(n)
    per_seg = tm // n
    i = pl.program_id(0)
    slot = i % 2
    prev = 1 - slot

    def idx_copy(tile):
        return pltpu.make_async_copy(dst_hbm.at[tile], idx_ref, idx_sem)

    def issue(seg):
        for r in range(seg * per_seg, (seg + 1) * per_seg):
            for k in range(TOP_K):
                row = idx_ref[r * TOP_K + k]
                pltpu.make_async_copy(y_hbm.at[pl.ds(row, pitch)], ybuf.at[slot, k, pl.ds(r * pitch, pitch)],
                                      row_sem.at[slot]).start(priority=k % 2)

    def wait_rows():
        pltpu.make_async_copy(ybuf.at[prev], ybuf.at[prev], row_sem.at[prev]).wait()
        for k in range(TOP_K):
            gate_ref[k] = jnp.broadcast_to(gates_ref[:, k:k + 1], (tm, LANES))

    def piece(j):
        cols = slice(j * LANES, (j + 1) * LANES)
        acc = alpha * h_ref[:, cols]
        for k in range(TOP_K):
            acc = acc + gate_ref[k] * ybuf[prev, k, pl.ds(j, tm, stride=pitch), :]
        r_ref[:, cols] = acc

    def finish():
        hn = _layer_norm(r_ref[...], g_ref[0], b_ref[0])
        ho_ref[...] = hn
        hbo_ref[...] = hn.astype(BF16)

    @pl.when(i == 0)
    def _():
        idx_copy(0).start()
        idx_copy(0).wait()
        for seg in range(n):
            issue(seg)

    @pl.when(jnp.logical_and(i > 0, i < n_tiles))
    def _():
        idx_copy(i).wait()
        wait_rows()
        for seg in range(n):
            issue(seg)
            piece(seg)

    @pl.when(i + 1 < n_tiles)
    def _():
        idx_copy(i + 1).start()

    @pl.when(i == n_tiles)
    def _():
        wait_rows()
        for j in range(n):
            piece(j)

    @pl.when(i > 0)
    def _():
        finish()


def _combine(dest_tiles, y_sorted, gates_t, h, g, b, layer, alpha):
    T, D = h.shape
    n_tiles, ktm = dest_tiles.shape
    tm = ktm // TOP_K

    def tile(i):
        return (jnp.maximum(i - 1, 0), 0)

    row = pl.BlockSpec((tm, D), tile)
    vec = pl.BlockSpec((1, 1, D), lambda i: (layer, 0, 0))
    kern = functools.partial(_combine_kernel, n_tiles=n_tiles, alpha=alpha)
    return pl.pallas_call(
        kern, grid=(n_tiles + 1,),
        in_specs=[pl.BlockSpec(memory_space=pl.ANY), pl.BlockSpec(memory_space=pl.ANY),
                  pl.BlockSpec((tm, TOP_K), tile), row, vec, vec],
        out_specs=[row, row],
        out_shape=[jax.ShapeDtypeStruct((T, D), F32), jax.ShapeDtypeStruct((T, D), BF16)],
        scratch_shapes=[pltpu.SMEM((ktm,), jnp.int32),
                        pltpu.VMEM((2, TOP_K, tm * _row_pitch(D // LANES), LANES), F32),
                        pltpu.VMEM((TOP_K, tm, LANES), F32), pltpu.VMEM((tm, D), F32),
                        pltpu.SemaphoreType.DMA, pltpu.SemaphoreType.DMA((2,))],
        compiler_params=_params("arbitrary"), name="moe_combine_ln2")(
            dest_tiles, y_sorted, gates_t, h, _rows(g), _rows(b))


def _routing_plan(top_e, rank, counts, n_blocks):
    T = top_e.shape[1]
    bm = MOE_BLOCK
    experts = jnp.arange(N_EXPERTS, dtype=jnp.int32)
    padded = (counts + bm - 1) // bm * bm
    pad_end = jnp.cumsum(padded)
    pad_start = pad_end - padded
    start_of = jnp.sum(jnp.where(top_e[None] == experts[:, None, None], pad_start[:, None, None], 0), axis=0)
    dest = start_of + rank
    blk_start = jnp.arange(n_blocks, dtype=jnp.int32) * bm
    blk_e = jnp.minimum(jnp.sum(pad_end[None, :] <= blk_start[:, None], axis=1), N_EXPERTS - 1).astype(jnp.int32)
    hot = blk_e[:, None] == experts[None, :]
    blk_cnt = jnp.sum(jnp.where(hot, counts[None, :], 0), axis=1)
    blk_pad_start = jnp.sum(jnp.where(hot, pad_start[None, :], 0), axis=1)
    blk_n = jnp.clip(blk_cnt - (blk_start - blk_pad_start), 0, bm).astype(jnp.int32)
    tm = ROUTE_TILE
    dest_tiles = dest.T.reshape(T // tm, tm * TOP_K).astype(jnp.int32)
    return dest_tiles, blk_e, blk_n, (pad_start + counts).astype(jnp.int32), pad_end.astype(jnp.int32)


def kernel(x, mem, ln_in_g, ln_in_b, w_in, pool_w, pool_scale, ret_log_gamma, ret_gn_g, sgu_ln_g, sgu_ln_b, sgu_w, sgu_b, w_mem_kv, w_branch, w_gate, b_gate, w_out, ln1_g, ln1_b, w_router, b_router, w_up, b_up, w_down, b_down, ln2_g, ln2_b):
    B, S, D = x.shape
    depth = w_in.shape[0]
    n_mem = mem.shape[1]
    T = B * S
    alpha = (2 * depth) ** 0.25
    head_dim = ret_gn_g.shape[1] // N_RET_HEADS
    n_blocks = -(-(T * TOP_K + N_EXPERTS * (MOE_BLOCK - 1)) // MOE_BLOCK)

    pos = jnp.arange(S, dtype=F32)
    inv_freq = jnp.exp(-math.log(ROPE_BASE) * jnp.arange(0, head_dim, 2, dtype=F32) / head_dim)
    ang = pos[:, None] * inv_freq[None, :]
    cos, sin = jnp.cos(ang), jnp.sin(ang)
    cos2 = jnp.concatenate([cos, cos], axis=1)
    sin2 = jnp.concatenate([-sin, sin], axis=1)

    pool_wb, sgu_wb = pool_w.astype(BF16), sgu_w.astype(BF16)
    w_router_t = jnp.swapaxes(w_router, 1, 2).astype(BF16)
    E = w_up.shape[1]
    b_up_r = b_up.reshape(depth * E, 1, -1)
    b_down_r = b_down.reshape(depth * E, 1, D)
    pitch_x = _row_pitch(D // 2 // LANES)
    pitch_y = _row_pitch(D // LANES)

    mem_b = mem.reshape(B * n_mem, D).astype(BF16)
    h, hb = _ln_in(x.reshape(T, D), ln_in_g, ln_in_b)
    for l in range(depth):
        z, (w_gate_b, w_branch_b, w_out_b) = _matmul(hb, w_in, l, tm=1024, tn=1024, name="in_proj",
                                                     side_weights=(w_gate, w_branch, w_out))
        kv, _ = _matmul(mem_b, w_mem_kv, l, tm=B * n_mem, tn=1024, name="mem_kv")
        y_pool, y_sgu, y_mem = _local_branches(z, kv, pool_wb, pool_scale, sgu_ln_g, sgu_ln_b, sgu_wb, sgu_b,
                                               l, S, n_mem)
        ys = (y_pool, _retention(z, ret_log_gamma, ret_gn_g, cos2, sin2, l, B, S), y_sgu, y_mem)
        merged, (w_up_b, w_down_b) = _gate_merge(hb, ys, w_gate_b, b_gate, w_branch_b, l,
                                                 side_weights=(w_up, w_down))
        h, hp, top_e, gates, rank, cnt = _out_ln(merged, w_out_b, h, ln1_g, ln1_b, w_router_t, b_router, l, alpha)
        dest_tiles, blk_e, blk_n, pad_lo, pad_hi = _routing_plan(top_e, rank, cnt[:, 0], n_blocks)
        xs = _dispatch(hp, dest_tiles * pitch_x, pad_lo, pad_hi, n_blocks * MOE_BLOCK, pitch_x)
        y_sorted = _moe(xs, blk_e, blk_n, w_up_b, b_up_r, w_down_b, b_down_r, l)
        h, hb = _combine(dest_tiles * pitch_y, y_sorted, gates.T, h, ln2_g, ln2_b, l, alpha)
    return h.reshape(B, S, D)
```

```python
import functools
import math

import jax
import jax.numpy as jnp
from jax import lax
from jax.experimental import pallas as pl
from jax.experimental.pallas import tpu as pltpu

F32 = jnp.float32
BF16 = jnp.bfloat16
U32 = jnp.uint32

N_BRANCHES = 4
POOL_WINDOWS = (2, 4, 8, 16)
N_RET_HEADS = 8
RET_CHUNK = 128
ROPE_BASE = 10000.0
SGU_CHUNK = 128
N_SGU_GROUPS = 4
N_MEM_HEADS = 4
N_EXPERTS = 32
TOP_K = 4
SWIGLU_LIMIT = 7.0
SWIGLU_ALPHA = 1.702
LN_EPS = 1e-5

V7X_VMEM_BYTES = 64 * 1024 * 1024
VMEM_LIMIT_BYTES = V7X_VMEM_BYTES - 8 * 1024 * 1024
BF16_SUBLANE_TILE = 16
LANES = 128

MOE_BLOCK = 256
ROUTE_TILE = 128

def _params(*semantics):
    return pltpu.CompilerParams(dimension_semantics=semantics, vmem_limit_bytes=VMEM_LIMIT_BYTES)


def _rows(v):
    return v.reshape(v.shape[0], 1, v.shape[1])


def _layer_norm(x, g, b):
    mu = jnp.mean(x, axis=-1, keepdims=True)
    xc = x - mu
    var = jnp.mean(xc * xc, axis=-1, keepdims=True)
    return xc * lax.rsqrt(var + LN_EPS) * g + b


def _pack_bf16_pairs(x):
    n = x.shape[1] // 2
    xb = x.astype(BF16).astype(F32)
    lo = lax.bitcast_convert_type(xb[:, :n], U32)
    hi = lax.bitcast_convert_type(xb[:, n:], U32)
    return (lo >> 16) | hi


def _row_pitch(n):
    assert n % 2 == 0
    return n + 1


def _store_row_major(ref, x, rows):
    n = x.shape[1] // LANES
    pitch = _row_pitch(n)
    for j in range(n):
        ref[pl.ds(j, rows, stride=pitch), :] = x[:, j * LANES:(j + 1) * LANES]
    ref[pl.ds(n, rows, stride=pitch), :] = jnp.zeros((rows, LANES), ref.dtype)


def _load_row_major(ref, rows, n, index=()):
    return [ref[(*index, pl.ds(j, rows, stride=_row_pitch(n)), slice(None))] for j in range(n)]


def _unpack_bf16_pairs(p):
    lo = lax.bitcast_convert_type(p << 16, F32).astype(BF16)
    hi = lax.bitcast_convert_type(p & jnp.uint32(0xFFFF0000), F32).astype(BF16)
    return jnp.concatenate([lo, hi], axis=1)


def _ln_in_kernel(x_ref, g_ref, b_ref, h_ref, hb_ref):
    h = _layer_norm(x_ref[...], g_ref[...], b_ref[...])
    h_ref[...] = h
    hb_ref[...] = h.astype(BF16)


def _ln_in(x, g, b, tm=512):
    T, D = x.shape
    row = pl.BlockSpec((tm, D), lambda i: (i, 0))
    vec = pl.BlockSpec((1, D), lambda i: (0, 0))
    return pl.pallas_call(
        _ln_in_kernel, grid=(T // tm,), in_specs=[row, vec, vec], out_specs=[row, row],
        out_shape=[jax.ShapeDtypeStruct((T, D), F32), jax.ShapeDtypeStruct((T, D), BF16)],
        compiler_params=_params("parallel"), name="ln_in")(x, g.reshape(1, D), b.reshape(1, D))


class _SideCasts:
    def __init__(self, weights, layer, n_steps, step_of):
        self.arrays, self.in_specs, self.out_specs, self.out_shapes, self.shapes = [], [], [], [], []
        for w in weights:
            cols = w.shape[-1]
            rows = math.prod(w.shape[1:-1])
            slab = rows // n_steps
            assert slab * n_steps == rows and slab % BF16_SUBLANE_TILE == 0, (w.shape, n_steps)
            self.arrays.append(w.reshape(w.shape[0] * rows, cols))
            self.in_specs.append(pl.BlockSpec((slab, cols), lambda *g: (layer * n_steps + step_of(*g), 0)))
            self.out_specs.append(pl.BlockSpec((slab, cols), lambda *g: (step_of(*g), 0)))
            self.out_shapes.append(jax.ShapeDtypeStruct((rows, cols), BF16))
            self.shapes.append(w.shape[1:])

    def __len__(self):
        return len(self.arrays)

    @staticmethod
    def run(in_refs, out_refs):
        for i_ref, o_ref in zip(in_refs, out_refs):
            o_ref[...] = i_ref[...].astype(BF16)

    def unflatten(self, outs):
        return [o.reshape(s) for o, s in zip(outs, self.shapes)]


def _matmul_kernel(a_ref, w_ref, *refs, n_side):
    side_in, o_ref, side_out, wb_ref = refs[:n_side], refs[n_side], refs[n_side + 1:-1], refs[-1]

    @pl.when(pl.program_id(1) == 0)
    def _():
        wb_ref[...] = w_ref[0].astype(BF16)

    out = jnp.dot(a_ref[...], wb_ref[...], preferred_element_type=F32).astype(o_ref.dtype)
    if len(o_ref.shape) == 2:
        o_ref[...] = out
    else:
        for p in range(o_ref.shape[0]):
            o_ref[p] = out[:, p * LANES:(p + 1) * LANES]
    _SideCasts.run(side_in, side_out)


def _matmul(a, w, layer, tm, tn, name, side_weights=(), piece_major=False):
    M, K = a.shape
    N = w.shape[2]
    n_i = M // tm
    side = _SideCasts(side_weights, layer, (N // tn) * n_i, lambda j, i: j * n_i + i)
    if piece_major:
        out_spec = pl.BlockSpec((tn // LANES, tm, LANES), lambda j, i: (j, i, 0))
        out_shape = jax.ShapeDtypeStruct((N // LANES, M, LANES), BF16)
    else:
        out_spec = pl.BlockSpec((tm, tn), lambda j, i: (i, j))
        out_shape = jax.ShapeDtypeStruct((M, N), BF16)
    outs = pl.pallas_call(
        functools.partial(_matmul_kernel, n_side=len(side)), grid=(N // tn, n_i),
        in_specs=[pl.BlockSpec((tm, K), lambda j, i: (i, 0)), pl.BlockSpec((1, K, tn), lambda j, i: (layer, 0, j)),
                  *side.in_specs],
        out_specs=[out_spec, *side.out_specs],
        out_shape=[out_shape, *side.out_shapes],
        scratch_shapes=[pltpu.VMEM((K, tn), BF16)],
        compiler_params=_params("arbitrary", "arbitrary"), name=name)(a, w, *side.arrays)
    return outs[0], side.unflatten(outs[1:])


POOL_HALO = 64


def _pool_bands(ts):
    r = jnp.arange(ts)[:, None] + POOL_HALO
    c = jnp.arange(ts + 2 * POOL_HALO)[None, :]
    return jnp.stack([((c >= r - w // 2) & (c < r + w // 2)) for w in POOL_WINDOWS]).astype(BF16)


def _pool_kernel(z_ref, prev_ref, next_ref, band_ref, w_ref, scale_ref, o_ref, ext_ref, *, seq, ts):
    halo = POOL_HALO
    gd = w_ref.shape[2]
    pos0 = (pl.program_id(0) % (seq // ts)) * ts
    for p in range(z_ref.shape[0]):
        lanes = slice(p * LANES, (p + 1) * LANES)
        ext_ref[0:halo, lanes] = jnp.where(pos0 == 0, jnp.zeros_like(prev_ref[p]), prev_ref[p])
        ext_ref[halo:halo + ts, lanes] = z_ref[p]
        ext_ref[halo + ts:, lanes] = jnp.where(pos0 + ts == seq, jnp.zeros_like(next_ref[p]), next_ref[p])
    pos = pos0 + lax.broadcasted_iota(jnp.int32, (ts, gd), 0)
    for g, w in enumerate(POOL_WINDOWS):
        cols = slice(g * gd, (g + 1) * gd)
        half = w // 2
        acc = jnp.dot(band_ref[g], ext_ref[:, cols], preferred_element_type=F32)
        cnt = jnp.minimum(pos + half, seq) - jnp.maximum(pos - half, 0)
        mixed = acc / cnt.astype(F32) - ext_ref[halo:halo + ts, cols].astype(F32)
        y = jnp.dot(mixed.astype(BF16), w_ref[0, g], preferred_element_type=F32)
        o_ref[:, cols] = (y * scale_ref[0, :, cols]).astype(o_ref.dtype)


def _retention_kernel(lg_ref, q_ref, k_ref, v_ref, g_ref, cos_ref, sin_ref, gn_ref, o_ref,
                      rs_ref, kr_ref, kvf_ref, *, seq, unroll_bwd, unroll_fwd):
    C = RET_CHUNK
    d = q_ref.shape[2]
    nc = seq // C
    head = pl.program_id(0) % N_RET_HEADS
    lgf = lg_ref[0, head]
    lgb = lg_ref[1, head]
    row = lax.broadcasted_iota(jnp.int32, (C, C), 0).astype(F32)
    col = lax.broadcasted_iota(jnp.int32, (C, C), 1).astype(F32)
    diff = row - col
    decay = jnp.where(diff >= 0, jnp.exp(lgf * jnp.maximum(diff, 0.0)), jnp.exp(lgb * jnp.maximum(-diff, 0.0)))
    rowd = lax.broadcasted_iota(jnp.int32, (C, d), 0).astype(F32)
    qwf = jnp.exp(lgf * (rowd + 1.0))
    qwb = jnp.exp(lgb * (C - rowd))
    kwf = jnp.exp(lgf * (C - 1.0 - rowd))
    kwb = jnp.exp(lgb * rowd)
    chunk_decay_f = jnp.exp(lgf * jnp.full((d, d), float(C), F32))
    chunk_decay_b = jnp.exp(lgb * jnp.full((d, d), float(C), F32))
    tn_dims = (((0,), (0,)), ((), ()))
    nt_dims = (((1,), (1,)), ((), ()))

    def rotary(ref, sl):
        x = ref[0, sl, :].astype(F32)
        return x * cos_ref[sl, :] + pltpu.roll(x, d // 2, axis=1) * sin_ref[sl, :]

    def chunk(n):
        return pl.ds(pl.multiple_of(n * C, C), C)

    def bwd_body(j, state):
        n = nc - 1 - j
        sl = chunk(n)
        rs_ref[n] = state.astype(BF16)
        k = rotary(k_ref, sl) * (d ** -0.5)
        kr_ref[sl, :] = k.astype(BF16)
        kw = jnp.concatenate([k * kwf, k * kwb], axis=1).astype(BF16)
        kv = lax.dot_general(kw, v_ref[0, sl, :], tn_dims, preferred_element_type=F32)
        kvf_ref[n] = kv[:d]
        return state * chunk_decay_b + kv[d:]

    lax.fori_loop(0, nc, bwd_body, jnp.zeros((d, d), F32), unroll=unroll_bwd)

    def fwd_body(n, state):
        sl = chunk(n)
        qb = rotary(q_ref, sl).astype(BF16)
        v = v_ref[0, sl, :]
        scores = lax.dot_general(qb, kr_ref[sl, :], nt_dims, preferred_element_type=F32) * decay
        y = jnp.dot(scores.astype(BF16), v, preferred_element_type=F32)
        states = jnp.concatenate([state.astype(BF16), rs_ref[n]], axis=1)
        cross = jnp.dot(qb, states, preferred_element_type=F32)
        y = y + cross[:, :d] * qwf + cross[:, d:] * qwb
        mu = jnp.mean(y, axis=-1, keepdims=True)
        yc = y - mu
        var = jnp.mean(yc * yc, axis=-1, keepdims=True)
        yn = yc * lax.rsqrt(var + LN_EPS) * gn_ref[0]
        gate = g_ref[0, sl, :].astype(F32)
        o_ref[sl, :] = (yn * (gate * jax.nn.sigmoid(gate))).astype(o_ref.dtype)
        return state * chunk_decay_f + kvf_ref[n]

    lax.fori_loop(0, nc, fwd_body, jnp.zeros((d, d), F32), unroll=unroll_fwd)


def _retention(z, log_gamma, gn_g, cos2, sin2, layer, batch, seq, unroll_bwd=8, unroll_fwd=16):
    T = z.shape[1]
    H = N_RET_HEADS
    d = gn_g.shape[1] // H
    nc = seq // RET_CHUNK

    def col(split):
        return pl.BlockSpec((1, seq, d), lambda i: (split * H + i % H, i // H, 0))

    table = pl.BlockSpec((seq, d), lambda i: (0, 0))
    kern = functools.partial(_retention_kernel, seq=seq, unroll_bwd=unroll_bwd, unroll_fwd=unroll_fwd)
    return pl.pallas_call(
        kern, grid=(batch * H,),
        in_specs=[pl.BlockSpec(memory_space=pltpu.SMEM), col(1), col(2), col(3), col(4), table, table,
                  pl.BlockSpec((1, 1, d), lambda i: (layer, 0, i % H))],
        out_specs=pl.BlockSpec((seq, d), lambda i: (i // H, i % H)),
        out_shape=jax.ShapeDtypeStruct((T, H * d), BF16),
        scratch_shapes=[pltpu.VMEM((nc, d, d), BF16), pltpu.VMEM((seq, d), BF16), pltpu.VMEM((nc, d, d), F32)],
        compiler_params=_params("parallel"), name="retention")(
            log_gamma[layer], z, z, z, z, cos2, sin2, _rows(gn_g))


def _sgu_kernel(u_ref, v_ref, lng_ref, lnb_ref, w_ref, bias_ref, o_ref, *, chunks):
    P = SGU_CHUNK
    gd = o_ref.shape[1] // N_SGU_GROUPS

    def body(c, _):
        rows = pl.ds(pl.multiple_of(c * P, P), P)
        v = jnp.concatenate([v_ref[p, rows, :] for p in range(v_ref.shape[0])], axis=1)
        vn = _layer_norm(jax.nn.gelu(v.astype(F32)), lng_ref[0], lnb_ref[0]).astype(BF16)
        for g in range(N_SGU_GROUPS):
            cols = slice(g * gd, (g + 1) * gd)
            mixed = jnp.dot(w_ref[0, g], vn[:, cols], preferred_element_type=F32) + bias_ref[:, cols]
            ppg = gd // LANES
            u = jnp.concatenate([u_ref[p, rows, :] for p in range(g * ppg, (g + 1) * ppg)], axis=1)
            u = jax.nn.gelu(u.astype(F32))
            o_ref[rows, cols] = (u * mixed).astype(o_ref.dtype)
        return 0

    lax.fori_loop(0, chunks, body, 0, unroll=True)


def _mem_attn_kernel(q_ref, kv_ref, o_ref):
    width = o_ref.shape[1]
    hd = width // N_MEM_HEADS
    nt_dims = (((1,), (1,)), ((), ()))
    for h in range(N_MEM_HEADS):
        cols = slice(h * hd, (h + 1) * hd)
        pph = hd // LANES
        q = jnp.concatenate([q_ref[p] for p in range(h * pph, (h + 1) * pph)], axis=1)
        s = lax.dot_general(q, kv_ref[:, cols], nt_dims, preferred_element_type=F32) * (hd ** -0.5)
        p = jnp.exp(s - jnp.max(s, axis=-1, keepdims=True))
        p = p / jnp.sum(p, axis=-1, keepdims=True)
        v = kv_ref[:, width + h * hd:width + (h + 1) * hd]
        o_ref[:, cols] = jnp.dot(p.astype(BF16), v, preferred_element_type=F32).astype(o_ref.dtype)


def _local_branches_kernel(a_ref, prev_ref, next_ref, band_ref, pw_ref, ps_ref, u_ref, v_ref, lng_ref, lnb_ref,
                           sw_ref, bias_ref, q_ref, kv_ref, pool_ref, sgu_ref, mem_ref, ext_ref, *, seq, ts):
    _pool_kernel(a_ref, prev_ref, next_ref, band_ref, pw_ref, ps_ref, pool_ref, ext_ref, seq=seq, ts=ts)
    _sgu_kernel(u_ref, v_ref, lng_ref, lnb_ref, sw_ref, bias_ref, sgu_ref, chunks=ts // SGU_CHUNK)
    _mem_attn_kernel(q_ref, kv_ref, mem_ref)


def _local_branches(z, kv, pool_w, pool_scale, sgu_ln_g, sgu_ln_b, sgu_w, sgu_b, layer, seq, n_mem, ts=512):
    T = z.shape[1]
    _, G, gd, _ = pool_w.shape
    width = G * gd
    pieces = width // LANES
    _, SG, P, _ = sgu_w.shape
    halo = POOL_HALO
    n_halo = T // halo
    bias = jnp.repeat(sgu_b[layer].T.astype(F32), width // SG, axis=1)
    kern = functools.partial(_local_branches_kernel, seq=seq, ts=ts)

    def rows(split):
        return pl.BlockSpec((pieces, ts, LANES), lambda i: (split, i, 0))

    vec = pl.BlockSpec((1, 1, width), lambda i: (layer, 0, 0))
    out = jax.ShapeDtypeStruct((T, width), BF16)
    out_rows = pl.BlockSpec((ts, width), lambda i: (i, 0))
    return pl.pallas_call(
        kern, grid=(T // ts,),
        in_specs=[
            rows(0),
            pl.BlockSpec((pieces, halo, LANES), lambda i: (0, jnp.maximum(i * (ts // halo) - 1, 0), 0)),
            pl.BlockSpec((pieces, halo, LANES), lambda i: (0, jnp.minimum((i + 1) * (ts // halo), n_halo - 1), 0)),
            pl.BlockSpec((G, ts, ts + 2 * halo), lambda i: (0, 0, 0)),
            pl.BlockSpec((1, G, gd, gd), lambda i: (layer, 0, 0, 0)), vec,
            rows(5), rows(6), vec, vec, pl.BlockSpec((1, SG, P, P), lambda i: (layer, 0, 0, 0)),
            pl.BlockSpec((P, width), lambda i: (0, 0)),
            rows(7), pl.BlockSpec((n_mem, 2 * width), lambda i: (i // (seq // ts), 0)),
        ],
        out_specs=[out_rows, out_rows, out_rows],
        out_shape=[out, out, out],
        scratch_shapes=[pltpu.VMEM((ts + 2 * halo, width), BF16)],
        compiler_params=_params("parallel"), name="local_branches")(
            z, z, z, _pool_bands(ts), pool_w, _rows(pool_scale), z, z, _rows(sgu_ln_g), _rows(sgu_ln_b), sgu_w, bias,
            z, kv)


def _gate_merge_kernel(hb_ref, y0_ref, y1_ref, y2_ref, y3_ref, wg_ref, bg_ref, wb_ref, *refs, n_side):
    side_in, o_ref, side_out = refs[:n_side], refs[n_side], refs[n_side + 1:]
    acc = None
    for b, y_ref in enumerate((y0_ref, y1_ref, y2_ref, y3_ref)):
        gate = jax.nn.sigmoid(jnp.dot(hb_ref[...], wg_ref[b], preferred_element_type=F32) + bg_ref[0, b:b + 1, :])
        term = gate * jnp.dot(y_ref[...], wb_ref[b], preferred_element_type=F32)
        acc = term if acc is None else acc + term
    o_ref[...] = acc.astype(o_ref.dtype)
    _SideCasts.run(side_in, side_out)


def _gate_merge(hb, ys, w_gate, b_gate, w_branch, layer, side_weights=(), tm=512, tn=512):
    T, D = hb.shape
    bw = w_branch.shape[1]
    n_i = T // tm
    side = _SideCasts(side_weights, layer, (D // tn) * n_i, lambda j, i: j * n_i + i)
    yspec = pl.BlockSpec((tm, bw), lambda j, i: (i, 0))
    once = pl.Buffered(1)
    outs = pl.pallas_call(
        functools.partial(_gate_merge_kernel, n_side=len(side)), grid=(D // tn, n_i),
        in_specs=[pl.BlockSpec((tm, D), lambda j, i: (i, 0)), yspec, yspec, yspec, yspec,
                  pl.BlockSpec((N_BRANCHES, D, tn), lambda j, i: (0, 0, j), pipeline_mode=once),
                  pl.BlockSpec((1, N_BRANCHES, tn), lambda j, i: (layer, 0, j)),
                  pl.BlockSpec((N_BRANCHES, bw, tn), lambda j, i: (0, 0, j), pipeline_mode=once),
                  *side.in_specs],
        out_specs=[pl.BlockSpec((tm, tn), lambda j, i: (i, j)), *side.out_specs],
        out_shape=[jax.ShapeDtypeStruct((T, D), BF16), *side.out_shapes],
        compiler_params=_params("arbitrary", "arbitrary"), name="gate_merge")(
            hb, *ys, w_gate, b_gate, w_branch, *side.arrays)
    return outs[0], side.unflatten(outs[1:])


def _route(hb, wr_ref, br_ref, e_ref, gate_ref, rank_ref, cnt_ref, carry_ref, first):
    tm = hb.shape[0]

    @pl.when(first)
    def _():
        carry_ref[...] = jnp.zeros_like(carry_ref)

    nt_dims = (((1,), (1,)), ((), ()))
    logits = lax.dot_general(wr_ref[0], hb, nt_dims, preferred_element_type=F32) + br_ref[0]
    eidx = lax.broadcasted_iota(jnp.int32, logits.shape, 0)
    work = logits
    tops, hots = [], []
    for k in range(TOP_K):
        m = jnp.max(work, axis=0, keepdims=True)
        sel = jnp.min(jnp.where(work == m, eidx, N_EXPERTS), axis=0, keepdims=True)
        hot = eidx == sel
        work = jnp.where(hot, -jnp.inf, work)
        e_ref[k:k + 1, :] = sel
        tops.append(m)
        hots.append(hot)
    ex = [jnp.exp(t - tops[0]) for t in tops]
    den = ex[0] + ex[1] + ex[2] + ex[3]
    member = jnp.zeros(logits.shape, F32)
    for hot in hots:
        member = member + jnp.where(hot, 1.0, 0.0)
    tri = jnp.where(lax.broadcasted_iota(jnp.int32, (tm, tm), 0) <= lax.broadcasted_iota(jnp.int32, (tm, tm), 1),
                    1.0, 0.0).astype(BF16)
    incl = jnp.dot(member.astype(BF16), tri, preferred_element_type=F32)
    before = carry_ref[:, 0:1] + incl - member
    for k in range(TOP_K):
        gate_ref[k:k + 1, :] = ex[k] / den
        rank_ref[k:k + 1, :] = jnp.sum(jnp.where(hots[k], before, 0.0), axis=0, keepdims=True).astype(jnp.int32)
    carry_ref[...] = carry_ref[...] + incl[:, tm - 1:tm]
    cnt_ref[...] = carry_ref[...].astype(jnp.int32)


def _out_ln_kernel(m_ref, w_ref, h_ref, g_ref, b_ref, wr_ref, br_ref,
                   ho_ref, hpo_ref, e_ref, gate_ref, rank_ref, cnt_ref, carry_ref, hb_ref, *, alpha, n_tiles):
    tm = m_ref.shape[0]
    i = pl.program_id(0)
    slot = i % 2

    def project():
        mix = jnp.dot(m_ref[...], w_ref[...], preferred_element_type=F32)
        hn = _layer_norm(alpha * h_ref[...] + mix, g_ref[0], b_ref[0])
        ho_ref[...] = hn
        _store_row_major(hpo_ref, _pack_bf16_pairs(hn), tm)
        hb_ref[slot] = hn.astype(BF16)

    def route():
        _route(hb_ref[1 - slot], wr_ref, br_ref, e_ref, gate_ref, rank_ref, cnt_ref, carry_ref, first=i == 1)

    @pl.when(i < n_tiles)
    def _():
        project()

    @pl.when(i > 0)
    def _():
        route()


def _out_ln(merged, w_out, h, g, b, w_router_t, b_router, layer, alpha, tm=256):
    T, D = h.shape
    E = w_router_t.shape[1]
    pieces = _row_pitch(D // 2 // LANES)
    n_tiles = T // tm

    def cur(i):
        return (jnp.minimum(i, n_tiles - 1), 0)

    row = pl.BlockSpec((tm, D), cur)
    vec = pl.BlockSpec((1, 1, D), lambda i: (layer, 0, 0))
    tok = pl.BlockSpec((TOP_K, tm), lambda i: (0, jnp.maximum(i - 1, 0)))
    kern = functools.partial(_out_ln_kernel, alpha=alpha, n_tiles=n_tiles)
    return pl.pallas_call(
        kern, grid=(n_tiles + 1,),
        in_specs=[row, pl.BlockSpec((D, D), lambda i: (0, 0)), row, vec, vec,
                  pl.BlockSpec((1, E, D), lambda i: (layer, 0, 0)), pl.BlockSpec((1, E, 1), lambda i: (layer, 0, 0))],
        out_specs=[row, pl.BlockSpec((tm * pieces, LANES), cur), tok, tok, tok,
                   pl.BlockSpec((E, LANES), lambda i: (0, 0))],
        out_shape=[jax.ShapeDtypeStruct((T, D), F32), jax.ShapeDtypeStruct((T * pieces, LANES), U32),
                   jax.ShapeDtypeStruct((TOP_K, T), jnp.int32), jax.ShapeDtypeStruct((TOP_K, T), F32),
                   jax.ShapeDtypeStruct((TOP_K, T), jnp.int32), jax.ShapeDtypeStruct((E, LANES), jnp.int32)],
        scratch_shapes=[pltpu.VMEM((E, LANES), F32), pltpu.VMEM((2, tm, D), BF16)],
        compiler_params=_params("arbitrary"), name="out_ln1_router")(
            merged, w_out, h, _rows(g), _rows(b), w_router_t, b_router.reshape(b_router.shape[0], E, 1))


def _dispatch_kernel(plo_ref, phi_ref, dst_hbm, hp_ref, xs_hbm, idx_ref, zero_ref, idx_sem, row_sem, zero_sem,
                     *, n_tiles, n_blocks, n):
    assert hp_ref.shape[0] % n == 0 and zero_ref.shape[0] % n == 0
    tm = hp_ref.shape[0] // n
    blk_rows = zero_ref.shape[0]
    bm = blk_rows // n
    i = pl.program_id(0)

    def idx_copy(tile):
        return pltpu.make_async_copy(dst_hbm.at[tile], idx_ref, idx_sem)

    def zero_row(r):
        return pltpu.make_async_copy(zero_ref.at[pl.ds(0, n)], xs_hbm.at[pl.ds(r * n, n)], zero_sem)

    def zero_block(blk):
        return pltpu.make_async_copy(
            zero_ref, xs_hbm.at[pl.ds(pl.multiple_of(blk * blk_rows, 8), blk_rows)], zero_sem)

    def for_each(lo, hi, fn):
        def body(r, _):
            fn(r)
            return 0
        lax.fori_loop(lo, hi, body, 0)

    @pl.when(i == 0)
    def _():
        idx_copy(0).start()
        zero_ref[...] = jnp.zeros_like(zero_ref)
        for e in range(N_EXPERTS):
            for_each(plo_ref[e], phi_ref[e], lambda r: zero_row(r).start())
            for_each(plo_ref[e], phi_ref[e], lambda r: zero_row(r).wait())
        first_unused = phi_ref[N_EXPERTS - 1] // bm
        for_each(first_unused, n_blocks, lambda blk: zero_block(blk).start())
        for_each(first_unused, n_blocks, lambda blk: zero_block(blk).wait())

    idx_copy(i).wait()
    for r in range(tm):
        for k in range(TOP_K):
            d = idx_ref[r * TOP_K + k]
            pltpu.make_async_copy(hp_ref.at[pl.ds(r * n, n)], xs_hbm.at[pl.ds(d, n)], row_sem).start(priority=k % 2)

    @pl.when(i + 1 < n_tiles)
    def _():
        idx_copy(i + 1).start()

    for k in range(TOP_K):
        pltpu.make_async_copy(hp_ref, xs_hbm.at[pl.ds(0, tm * n)], row_sem).wait()


def _dispatch(hp, dest_tiles, pad_lo, pad_hi, n_slots, n):
    n_tiles, ktm = dest_tiles.shape
    tm = ktm // TOP_K
    kern = functools.partial(_dispatch_kernel, n_tiles=n_tiles, n_blocks=n_slots // MOE_BLOCK, n=n)
    grid_spec = pltpu.PrefetchScalarGridSpec(
        num_scalar_prefetch=2, grid=(n_tiles,),
        in_specs=[pl.BlockSpec(memory_space=pl.ANY), pl.BlockSpec((tm * n, LANES), lambda i, lo, hi: (i, 0))],
        out_specs=pl.BlockSpec(memory_space=pl.ANY),
        scratch_shapes=[pltpu.SMEM((ktm,), jnp.int32), pltpu.VMEM((MOE_BLOCK * n, LANES), U32),
                        pltpu.SemaphoreType.DMA, pltpu.SemaphoreType.DMA, pltpu.SemaphoreType.DMA])
    return pl.pallas_call(
        kern, grid_spec=grid_spec, out_shape=jax.ShapeDtypeStruct((n_slots * n, LANES), U32),
        compiler_params=_params("arbitrary"), name="moe_dispatch")(pad_lo, pad_hi, dest_tiles, hp)


def _moe_kernel(be_ref, bn_ref, x_ref, wup_ref, bup_ref, wdn_ref, bdn_ref, y_ref, *, bm):
    ed = wdn_ref.shape[1]
    n_x = wup_ref.shape[1] // 2 // LANES
    i = pl.program_id(0)

    def ffn(rows):
        x = _unpack_bf16_pairs(jnp.concatenate(_load_row_major(x_ref, rows, n_x), axis=1))
        gu = jnp.dot(x, wup_ref[0], preferred_element_type=F32) + bup_ref[0]
        glu = jnp.minimum(gu[:, :ed], SWIGLU_LIMIT)
        lin = jnp.clip(gu[:, ed:], -SWIGLU_LIMIT, SWIGLU_LIMIT)
        act = glu * jax.nn.sigmoid(SWIGLU_ALPHA * glu) * (lin + 1.0)
        y = jnp.dot(act.astype(BF16), wdn_ref[0], preferred_element_type=F32) + bdn_ref[0]
        _store_row_major(y_ref, y, rows)
        if rows < bm:
            rest = y_ref.shape[0] // bm * rows
            y_ref[rest:, :] = jnp.zeros((y_ref.shape[0] - rest, LANES), y_ref.dtype)

    @pl.when(bn_ref[i] > bm // 2)
    def _():
        ffn(bm)

    @pl.when(jnp.logical_and(bn_ref[i] > 0, bn_ref[i] <= bm // 2))
    def _():
        ffn(bm // 2)

    @pl.when(bn_ref[i] == 0)
    def _():
        y_ref[...] = jnp.zeros_like(y_ref)


def _moe(xs, blk_e, blk_n, w_up, b_up, w_down, b_down, layer):
    bm = MOE_BLOCK
    _, D, two_ed = w_up.shape
    ed = two_ed // 2
    n_x = _row_pitch(D // 2 // LANES)
    n_y = _row_pitch(D // LANES)
    n_blocks = xs.shape[0] // (bm * n_x)
    base = layer * N_EXPERTS

    def expert(i, be, bn):
        return (be[i], 0, 0)

    def expert_bias(i, be, bn):
        return (base + be[i], 0, 0)

    grid_spec = pltpu.PrefetchScalarGridSpec(
        num_scalar_prefetch=2, grid=(n_blocks,),
        in_specs=[pl.BlockSpec((bm * n_x, LANES), lambda i, be, bn: (jnp.where(bn[i] > 0, i, 0), 0)),
                  pl.BlockSpec((1, D, two_ed), expert), pl.BlockSpec((1, 1, two_ed), expert_bias),
                  pl.BlockSpec((1, ed, D), expert), pl.BlockSpec((1, 1, D), expert_bias)],
        out_specs=pl.BlockSpec((bm * n_y, LANES), lambda i, be, bn: (i, 0)))
    return pl.pallas_call(
        functools.partial(_moe_kernel, bm=bm), grid_spec=grid_spec,
        out_shape=jax.ShapeDtypeStruct((n_blocks * bm * n_y, LANES), F32),
        compiler_params=_params("arbitrary"), name="moe_experts")(blk_e, blk_n, xs, w_up, b_up, w_down, b_down)


def _combine_kernel(dst_hbm, y_hbm, gates_ref, h_ref, g_ref, b_ref, ho_ref, hbo_ref,
                    idx_ref, ybuf, gate_ref, r_ref, idx_sem, row_sem, *, n_tiles, alpha):
    tm, D = h_ref.shape
    n = D // LANES
    pitch = _row_pitch(n)
    per_seg = tm // n
    i = pl.program_id(0)
    slot = i % 2
    prev = 1 - slot

    def idx_copy(tile):
        return pltpu.make_async_copy(dst_hbm.at[tile], idx_ref, idx_sem)

    def issue(seg):
        for r in range(seg * per_seg, (seg + 1) * per_seg):
            for k in range(TOP_K):
                row = idx_ref[r * TOP_K + k]
                pltpu.make_async_copy(y_hbm.at[pl.ds(row, pitch)], ybuf.at[slot, k, pl.ds(r * pitch, pitch)],
                                      row_sem.at[slot]).start(priority=k % 2)

    def wait_rows():
        pltpu.make_async_copy(ybuf.at[prev], ybuf.at[prev], row_sem.at[prev]).wait()
        for k in range(TOP_K):
            gate_ref[k] = jnp.broadcast_to(gates_ref[:, k:k + 1], (tm, LANES))

    def piece(j):
        cols = slice(j * LANES, (j + 1) * LANES)
        acc = alpha * h_ref[:, cols]
        for k in range(TOP_K):
            acc = acc + gate_ref[k] * ybuf[prev, k, pl.ds(j, tm, stride=pitch), :]
        r_ref[:, cols] = acc

    def finish():
        hn = _layer_norm(r_ref[...], g_ref[0], b_ref[0])
        ho_ref[...] = hn
        hbo_ref[...] = hn.astype(BF16)

    @pl.when(i == 0)
    def _():
        idx_copy(0).start()
        idx_copy(0).wait()
        for seg in range(n):
            issue(seg)

    @pl.when(jnp.logical_and(i > 0, i < n_tiles))
    def _():
        idx_copy(i).wait()
        wait_rows()
        for seg in range(n):
            issue(seg)
            piece(seg)

    @pl.when(i + 1 < n_tiles)
    def _():
        idx_copy(i + 1).start()

    @pl.when(i == n_tiles)
    def _():
        wait_rows()
        for j in range(n):
            piece(j)

    @pl.when(i > 0)
    def _():
        finish()


def _combine(dest_tiles, y_sorted, gates_t, h, g, b, layer, alpha):
    T, D = h.shape
    n_tiles, ktm = dest_tiles.shape
    tm = ktm // TOP_K

    def tile(i):
        return (jnp.maximum(i - 1, 0), 0)

    row = pl.BlockSpec((tm, D), tile)
    vec = pl.BlockSpec((1, 1, D), lambda i: (layer, 0, 0))
    kern = functools.partial(_combine_kernel, n_tiles=n_tiles, alpha=alpha)
    return pl.pallas_call(
        kern, grid=(n_tiles + 1,),
        in_specs=[pl.BlockSpec(memory_space=pl.ANY), pl.BlockSpec(memory_space=pl.ANY),
                  pl.BlockSpec((tm, TOP_K), tile), row, vec, vec],
        out_specs=[row, row],
        out_shape=[jax.ShapeDtypeStruct((T, D), F32), jax.ShapeDtypeStruct((T, D), BF16)],
        scratch_shapes=[pltpu.SMEM((ktm,), jnp.int32),
                        pltpu.VMEM((2, TOP_K, tm * _row_pitch(D // LANES), LANES), F32),
                        pltpu.VMEM((TOP_K, tm, LANES), F32), pltpu.VMEM((tm, D), F32),
                        pltpu.SemaphoreType.DMA, pltpu.SemaphoreType.DMA((2,))],
        compiler_params=_params("arbitrary"), name="moe_combine_ln2")(
            dest_tiles, y_sorted, gates_t, h, _rows(g), _rows(b))


def _routing_plan(top_e, rank, counts, n_blocks):
    T = top_e.shape[1]
    bm = MOE_BLOCK
    experts = jnp.arange(N_EXPERTS, dtype=jnp.int32)
    padded = (counts + bm - 1) // bm * bm
    pad_end = jnp.cumsum(padded)
    pad_start = pad_end - padded
    start_of = jnp.sum(jnp.where(top_e[None] == experts[:, None, None], pad_start[:, None, None], 0), axis=0)
    dest = start_of + rank
    blk_start = jnp.arange(n_blocks, dtype=jnp.int32) * bm
    blk_e = jnp.minimum(jnp.sum(pad_end[None, :] <= blk_start[:, None], axis=1), N_EXPERTS - 1).astype(jnp.int32)
    hot = blk_e[:, None] == experts[None, :]
    blk_cnt = jnp.sum(jnp.where(hot, counts[None, :], 0), axis=1)
    blk_pad_start = jnp.sum(jnp.where(hot, pad_start[None, :], 0), axis=1)
    blk_n = jnp.clip(blk_cnt - (blk_start - blk_pad_start), 0, bm).astype(jnp.int32)
    tm = ROUTE_TILE
    dest_tiles = dest.T.reshape(T // tm, tm * TOP_K).astype(jnp.int32)
    return dest_tiles, blk_e, blk_n, (pad_start + counts).astype(jnp.int32), pad_end.astype(jnp.int32)


def kernel(x, mem, ln_in_g, ln_in_b, w_in, pool_w, pool_scale, ret_log_gamma, ret_gn_g, sgu_ln_g, sgu_ln_b, sgu_w, sgu_b, w_mem_kv, w_branch, w_gate, b_gate, w_out, ln1_g, ln1_b, w_router, b_router, w_up, b_up, w_down, b_down, ln2_g, ln2_b):
    B, S, D = x.shape
    depth = w_in.shape[0]
    n_mem = mem.shape[1]
    T = B * S
    alpha = (2 * depth) ** 0.25
    head_dim = ret_gn_g.shape[1] // N_RET_HEADS
    n_blocks = -(-(T * TOP_K + N_EXPERTS * (MOE_BLOCK - 1)) // MOE_BLOCK)

    pos = jnp.arange(S, dtype=F32)
    inv_freq = jnp.exp(-math.log(ROPE_BASE) * jnp.arange(0, head_dim, 2, dtype=F32) / head_dim)
    ang = pos[:, None] * inv_freq[None, :]
    cos, sin = jnp.cos(ang), jnp.sin(ang)
    cos2 = jnp.concatenate([cos, cos], axis=1)
    sin2 = jnp.concatenate([-sin, sin], axis=1)

    pool_wb, sgu_wb = pool_w.astype(BF16), sgu_w.astype(BF16)
    w_router_t = jnp.swapaxes(w_router, 1, 2).astype(BF16)
    E = w_up.shape[1]
    b_up_r = b_up.reshape(depth * E, 1, -1)
    b_down_r = b_down.reshape(depth * E, 1, D)
    pitch_x = _row_pitch(D // 2 // LANES)
    pitch_y = _row_pitch(D // LANES)

    mem_b = mem.reshape(B * n_mem, D).astype(BF16)
    h, hb = _ln_in(x.reshape(T, D), ln_in_g, ln_in_b)
    for l in range(depth):
        z, (w_gate_b, w_branch_b, w_out_b) = _matmul(hb, w_in, l, tm=1024, tn=1024, name="in_proj",
                                                     side_weights=(w_gate, w_branch, w_out), piece_major=True)
        kv, _ = _matmul(mem_b, w_mem_kv, l, tm=B * n_mem, tn=1024, name="mem_kv")
        y_pool, y_sgu, y_mem = _local_branches(z, kv, pool_wb, pool_scale, sgu_ln_g, sgu_ln_b, sgu_wb, sgu_b,
                                               l, S, n_mem)
        ys = (y_pool, _retention(z, ret_log_gamma, ret_gn_g, cos2, sin2, l, B, S), y_sgu, y_mem)
        merged, (w_up_b, w_down_b) = _gate_merge(hb, ys, w_gate_b, b_gate, w_branch_b, l,
                                                 side_weights=(w_up, w_down))
        h, hp, top_e, gates, rank, cnt = _out_ln(merged, w_out_b, h, ln1_g, ln1_b, w_router_t, b_router, l, alpha)
        dest_tiles, blk_e, blk_n, pad_lo, pad_hi = _routing_plan(top_e, rank, cnt[:, 0], n_blocks)
        xs = _dispatch(hp, dest_tiles * pitch_x, pad_lo, pad_hi, n_blocks * MOE_BLOCK, pitch_x)
        y_sorted = _moe(xs, blk_e, blk_n, w_up_b, b_up_r, w_down_b, b_down_r, l)
        h, hb = _combine(dest_tiles * pitch_y, y_sorted, gates.T, h, ln2_g, ln2_b, l, alpha)
    return h.reshape(B, S, D)
```

```python
import functools
import math

import jax
import jax.numpy as jnp
from jax import lax
from jax.experimental import pallas as pl
from jax.experimental.pallas import tpu as pltpu

F32 = jnp.float32
BF16 = jnp.bfloat16
U32 = jnp.uint32

N_BRANCHES = 4
POOL_WINDOWS = (2, 4, 8, 16)
N_RET_HEADS = 8
RET_CHUNK = 128
ROPE_BASE = 10000.0
SGU_CHUNK = 128
N_SGU_GROUPS = 4
N_MEM_HEADS = 4
N_EXPERTS = 32
TOP_K = 4
SWIGLU_LIMIT = 7.0
SWIGLU_ALPHA = 1.702
LN_EPS = 1e-5

V7X_VMEM_BYTES = 64 * 1024 * 1024
VMEM_LIMIT_BYTES = V7X_VMEM_BYTES - 8 * 1024 * 1024
BF16_SUBLANE_TILE = 16
LANES = 128

MOE_BLOCK = 256
ROUTE_TILE = 128

def _params(*semantics):
    return pltpu.CompilerParams(dimension_semantics=semantics, vmem_limit_bytes=VMEM_LIMIT_BYTES)


def _rows(v):
    return v.reshape(v.shape[0], 1, v.shape[1])


def _layer_norm(x, g, b):
    mu = jnp.mean(x, axis=-1, keepdims=True)
    xc = x - mu
    var = jnp.mean(xc * xc, axis=-1, keepdims=True)
    return xc * lax.rsqrt(var + LN_EPS) * g + b


def _pack_bf16_pairs(x):
    n = x.shape[1] // 2
    xb = x.astype(BF16).astype(F32)
    lo = lax.bitcast_convert_type(xb[:, :n], U32)
    hi = lax.bitcast_convert_type(xb[:, n:], U32)
    return (lo >> 16) | hi


def _row_pitch(n):
    assert n % 2 == 0
    return n + 1


def _store_row_major(ref, x, rows):
    n = x.shape[1] // LANES
    pitch = _row_pitch(n)
    for j in range(n):
        ref[pl.ds(j, rows, stride=pitch), :] = x[:, j * LANES:(j + 1) * LANES]
    ref[pl.ds(n, rows, stride=pitch), :] = jnp.zeros((rows, LANES), ref.dtype)


def _load_row_major(ref, rows, n, index=()):
    return [ref[(*index, pl.ds(j, rows, stride=_row_pitch(n)), slice(None))] for j in range(n)]


def _unpack_bf16_pairs(p):
    lo = lax.bitcast_convert_type(p << 16, F32).astype(BF16)
    hi = lax.bitcast_convert_type(p & jnp.uint32(0xFFFF0000), F32).astype(BF16)
    return jnp.concatenate([lo, hi], axis=1)


def _ln_in_kernel(x_ref, g_ref, b_ref, h_ref, hb_ref):
    h = _layer_norm(x_ref[...], g_ref[...], b_ref[...])
    h_ref[...] = h
    hb_ref[...] = h.astype(BF16)


def _ln_in(x, g, b, tm=512):
    T, D = x.shape
    row = pl.BlockSpec((tm, D), lambda i: (i, 0))
    vec = pl.BlockSpec((1, D), lambda i: (0, 0))
    return pl.pallas_call(
        _ln_in_kernel, grid=(T // tm,), in_specs=[row, vec, vec], out_specs=[row, row],
        out_shape=[jax.ShapeDtypeStruct((T, D), F32), jax.ShapeDtypeStruct((T, D), BF16)],
        compiler_params=_params("parallel"), name="ln_in")(x, g.reshape(1, D), b.reshape(1, D))


class _SideCasts:
    def __init__(self, weights, layer, n_steps, step_of):
        self.arrays, self.in_specs, self.out_specs, self.out_shapes, self.shapes = [], [], [], [], []
        for w in weights:
            cols = w.shape[-1]
            rows = math.prod(w.shape[1:-1])
            slab = rows // n_steps
            assert slab * n_steps == rows and slab % BF16_SUBLANE_TILE == 0, (w.shape, n_steps)
            self.arrays.append(w.reshape(w.shape[0] * rows, cols))
            self.in_specs.append(pl.BlockSpec((slab, cols), lambda *g: (layer * n_steps + step_of(*g), 0)))
            self.out_specs.append(pl.BlockSpec((slab, cols), lambda *g: (step_of(*g), 0)))
            self.out_shapes.append(jax.ShapeDtypeStruct((rows, cols), BF16))
            self.shapes.append(w.shape[1:])

    def __len__(self):
        return len(self.arrays)

    @staticmethod
    def run(in_refs, out_refs):
        for i_ref, o_ref in zip(in_refs, out_refs):
            o_ref[...] = i_ref[...].astype(BF16)

    def unflatten(self, outs):
        return [o.reshape(s) for o, s in zip(outs, self.shapes)]


def _matmul_kernel(a_ref, w_ref, *refs, n_side):
    side_in, o_ref, side_out, wb_ref = refs[:n_side], refs[n_side], refs[n_side + 1:-1], refs[-1]

    @pl.when(pl.program_id(1) == 0)
    def _():
        wb_ref[...] = w_ref[0].astype(BF16)

    out = jnp.dot(a_ref[...], wb_ref[...], preferred_element_type=F32).astype(o_ref.dtype)
    if len(o_ref.shape) == 2:
        o_ref[...] = out
    else:
        for p in range(o_ref.shape[0]):
            o_ref[p] = out[:, p * LANES:(p + 1) * LANES]
    _SideCasts.run(side_in, side_out)


def _matmul(a, w, layer, tm, tn, name, side_weights=(), piece_major=False):
    M, K = a.shape
    N = w.shape[2]
    n_i = M // tm
    side = _SideCasts(side_weights, layer, (N // tn) * n_i, lambda j, i: j * n_i + i)
    if piece_major:
        out_spec = pl.BlockSpec((tn // LANES, tm, LANES), lambda j, i: (j, i, 0))
        out_shape = jax.ShapeDtypeStruct((N // LANES, M, LANES), BF16)
    else:
        out_spec = pl.BlockSpec((tm, tn), lambda j, i: (i, j))
        out_shape = jax.ShapeDtypeStruct((M, N), BF16)
    outs = pl.pallas_call(
        functools.partial(_matmul_kernel, n_side=len(side)), grid=(N // tn, n_i),
        in_specs=[pl.BlockSpec((tm, K), lambda j, i: (i, 0)), pl.BlockSpec((1, K, tn), lambda j, i: (layer, 0, j)),
                  *side.in_specs],
        out_specs=[out_spec, *side.out_specs],
        out_shape=[out_shape, *side.out_shapes],
        scratch_shapes=[pltpu.VMEM((K, tn), BF16)],
        compiler_params=_params("arbitrary", "arbitrary"), name=name)(a, w, *side.arrays)
    return outs[0], side.unflatten(outs[1:])


POOL_HALO = 64


def _pool_bands(ts):
    r = jnp.arange(ts)[:, None] + POOL_HALO
    c = jnp.arange(ts + 2 * POOL_HALO)[None, :]
    return jnp.stack([((c >= r - w // 2) & (c < r + w // 2)) for w in POOL_WINDOWS]).astype(BF16)


def _pool_kernel(z_ref, prev_ref, next_ref, band_ref, w_ref, scale_ref, o_ref, ext_ref, *, seq, ts):
    halo = POOL_HALO
    gd = w_ref.shape[2]
    pos0 = (pl.program_id(0) % (seq // ts)) * ts
    for p in range(z_ref.shape[0]):
        lanes = slice(p * LANES, (p + 1) * LANES)
        ext_ref[0:halo, lanes] = jnp.where(pos0 == 0, jnp.zeros_like(prev_ref[p]), prev_ref[p])
        ext_ref[halo:halo + ts, lanes] = z_ref[p]
        ext_ref[halo + ts:, lanes] = jnp.where(pos0 + ts == seq, jnp.zeros_like(next_ref[p]), next_ref[p])
    pos = pos0 + lax.broadcasted_iota(jnp.int32, (ts, gd), 0)
    for g, w in enumerate(POOL_WINDOWS):
        cols = slice(g * gd, (g + 1) * gd)
        half = w // 2
        acc = jnp.dot(band_ref[g], ext_ref[:, cols], preferred_element_type=F32)
        cnt = jnp.minimum(pos + half, seq) - jnp.maximum(pos - half, 0)
        mixed = acc / cnt.astype(F32) - ext_ref[halo:halo + ts, cols].astype(F32)
        y = jnp.dot(mixed.astype(BF16), w_ref[0, g], preferred_element_type=F32)
        o_ref[:, cols] = (y * scale_ref[0, :, cols]).astype(o_ref.dtype)


def _retention_kernel(lg_ref, q_ref, k_ref, v_ref, g_ref, cos_ref, sin_ref, gn_ref, o_ref,
                      rs_ref, kr_ref, kvf_ref, *, seq, unroll_bwd, unroll_fwd):
    C = RET_CHUNK
    d = q_ref.shape[2]
    nc = seq // C
    head = pl.program_id(0) % N_RET_HEADS
    lgf = lg_ref[0, head]
    lgb = lg_ref[1, head]
    row = lax.broadcasted_iota(jnp.int32, (C, C), 0).astype(F32)
    col = lax.broadcasted_iota(jnp.int32, (C, C), 1).astype(F32)
    diff = row - col
    decay = jnp.where(diff >= 0, jnp.exp(lgf * jnp.maximum(diff, 0.0)), jnp.exp(lgb * jnp.maximum(-diff, 0.0)))
    rowd = lax.broadcasted_iota(jnp.int32, (C, d), 0).astype(F32)
    qwf = jnp.exp(lgf * (rowd + 1.0))
    qwb = jnp.exp(lgb * (C - rowd))
    kwf = jnp.exp(lgf * (C - 1.0 - rowd))
    kwb = jnp.exp(lgb * rowd)
    chunk_decay_f = jnp.exp(lgf * jnp.full((d, d), float(C), F32))
    chunk_decay_b = jnp.exp(lgb * jnp.full((d, d), float(C), F32))
    tn_dims = (((0,), (0,)), ((), ()))
    nt_dims = (((1,), (1,)), ((), ()))

    def rotary(ref, sl):
        x = ref[0, sl, :].astype(F32)
        return x * cos_ref[sl, :] + pltpu.roll(x, d // 2, axis=1) * sin_ref[sl, :]

    def chunk(n):
        return pl.ds(pl.multiple_of(n * C, C), C)

    def bwd_body(j, state):
        n = nc - 1 - j
        sl = chunk(n)
        rs_ref[n] = state.astype(BF16)
        k = rotary(k_ref, sl) * (d ** -0.5)
        kr_ref[sl, :] = k.astype(BF16)
        kw = jnp.concatenate([k * kwf, k * kwb], axis=1).astype(BF16)
        kv = lax.dot_general(kw, v_ref[0, sl, :], tn_dims, preferred_element_type=F32)
        kvf_ref[n] = kv[:d]
        return state * chunk_decay_b + kv[d:]

    lax.fori_loop(0, nc, bwd_body, jnp.zeros((d, d), F32), unroll=unroll_bwd)

    def fwd_body(n, state):
        sl = chunk(n)
        qb = rotary(q_ref, sl).astype(BF16)
        v = v_ref[0, sl, :]
        scores = lax.dot_general(qb, kr_ref[sl, :], nt_dims, preferred_element_type=F32) * decay
        y = jnp.dot(scores.astype(BF16), v, preferred_element_type=F32)
        states = jnp.concatenate([state.astype(BF16), rs_ref[n]], axis=1)
        cross = jnp.dot(qb, states, preferred_element_type=F32)
        y = y + cross[:, :d] * qwf + cross[:, d:] * qwb
        mu = jnp.mean(y, axis=-1, keepdims=True)
        yc = y - mu
        var = jnp.mean(yc * yc, axis=-1, keepdims=True)
        yn = yc * lax.rsqrt(var + LN_EPS) * gn_ref[0]
        gate = g_ref[0, sl, :].astype(F32)
        o_ref[sl, :] = (yn * (gate * jax.nn.sigmoid(gate))).astype(o_ref.dtype)
        return state * chunk_decay_f + kvf_ref[n]

    lax.fori_loop(0, nc, fwd_body, jnp.zeros((d, d), F32), unroll=unroll_fwd)


def _retention(z, log_gamma, gn_g, cos2, sin2, layer, batch, seq, unroll_bwd=8, unroll_fwd=16):
    T = z.shape[1]
    H = N_RET_HEADS
    d = gn_g.shape[1] // H
    nc = seq // RET_CHUNK

    def col(split):
        return pl.BlockSpec((1, seq, d), lambda i: (split * H + i % H, i // H, 0))

    table = pl.BlockSpec((seq, d), lambda i: (0, 0))
    kern = functools.partial(_retention_kernel, seq=seq, unroll_bwd=unroll_bwd, unroll_fwd=unroll_fwd)
    return pl.pallas_call(
        kern, grid=(batch * H,),
        in_specs=[pl.BlockSpec(memory_space=pltpu.SMEM), col(1), col(2), col(3), col(4), table, table,
                  pl.BlockSpec((1, 1, d), lambda i: (layer, 0, i % H))],
        out_specs=pl.BlockSpec((seq, d), lambda i: (i // H, i % H)),
        out_shape=jax.ShapeDtypeStruct((T, H * d), BF16),
        scratch_shapes=[pltpu.VMEM((nc, d, d), BF16), pltpu.VMEM((seq, d), BF16), pltpu.VMEM((nc, d, d), F32)],
        compiler_params=_params("parallel"), name="retention")(
            log_gamma[layer], z, z, z, z, cos2, sin2, _rows(gn_g))


def _sgu_kernel(u_ref, v_ref, lng_ref, lnb_ref, w_ref, bias_ref, o_ref, *, chunks):
    P = SGU_CHUNK
    gd = o_ref.shape[1] // N_SGU_GROUPS

    def body(c, _):
        rows = pl.ds(pl.multiple_of(c * P, P), P)
        v = jnp.concatenate([v_ref[p, rows, :] for p in range(v_ref.shape[0])], axis=1)
        vn = _layer_norm(jax.nn.gelu(v.astype(F32)), lng_ref[0], lnb_ref[0]).astype(BF16)
        for g in range(N_SGU_GROUPS):
            cols = slice(g * gd, (g + 1) * gd)
            mixed = jnp.dot(w_ref[0, g], vn[:, cols], preferred_element_type=F32) + bias_ref[:, cols]
            ppg = gd // LANES
            u = jnp.concatenate([u_ref[p, rows, :] for p in range(g * ppg, (g + 1) * ppg)], axis=1)
            u = jax.nn.gelu(u.astype(F32))
            o_ref[rows, cols] = (u * mixed).astype(o_ref.dtype)
        return 0

    lax.fori_loop(0, chunks, body, 0, unroll=True)


def _mem_attn_kernel(q_ref, kv_ref, o_ref):
    width = o_ref.shape[1]
    hd = width // N_MEM_HEADS
    nt_dims = (((1,), (1,)), ((), ()))
    for h in range(N_MEM_HEADS):
        cols = slice(h * hd, (h + 1) * hd)
        pph = hd // LANES
        q = jnp.concatenate([q_ref[p] for p in range(h * pph, (h + 1) * pph)], axis=1)
        s = lax.dot_general(q, kv_ref[:, cols], nt_dims, preferred_element_type=F32) * (hd ** -0.5)
        p = jnp.exp(s - jnp.max(s, axis=-1, keepdims=True))
        p = p / jnp.sum(p, axis=-1, keepdims=True)
        v = kv_ref[:, width + h * hd:width + (h + 1) * hd]
        o_ref[:, cols] = jnp.dot(p.astype(BF16), v, preferred_element_type=F32).astype(o_ref.dtype)


def _local_branches_kernel(a_ref, prev_ref, next_ref, band_ref, pw_ref, ps_ref, u_ref, v_ref, lng_ref, lnb_ref,
                           sw_ref, bias_ref, q_ref, kv_ref, pool_ref, sgu_ref, mem_ref, ext_ref, *, seq, ts):
    _pool_kernel(a_ref, prev_ref, next_ref, band_ref, pw_ref, ps_ref, pool_ref, ext_ref, seq=seq, ts=ts)
    _sgu_kernel(u_ref, v_ref, lng_ref, lnb_ref, sw_ref, bias_ref, sgu_ref, chunks=ts // SGU_CHUNK)
    _mem_attn_kernel(q_ref, kv_ref, mem_ref)


def _local_branches(z, kv, pool_w, pool_scale, sgu_ln_g, sgu_ln_b, sgu_w, sgu_b, layer, seq, n_mem, ts=512):
    T = z.shape[1]
    _, G, gd, _ = pool_w.shape
    width = G * gd
    pieces = width // LANES
    _, SG, P, _ = sgu_w.shape
    halo = POOL_HALO
    n_halo = T // halo
    bias = jnp.repeat(sgu_b[layer].T.astype(F32), width // SG, axis=1)
    kern = functools.partial(_local_branches_kernel, seq=seq, ts=ts)

    def rows(split):
        return pl.BlockSpec((pieces, ts, LANES), lambda i: (split, i, 0))

    vec = pl.BlockSpec((1, 1, width), lambda i: (layer, 0, 0))
    out = jax.ShapeDtypeStruct((T, width), BF16)
    out_rows = pl.BlockSpec((ts, width), lambda i: (i, 0))
    return pl.pallas_call(
        kern, grid=(T // ts,),
        in_specs=[
            rows(0),
            pl.BlockSpec((pieces, halo, LANES), lambda i: (0, jnp.maximum(i * (ts // halo) - 1, 0), 0)),
            pl.BlockSpec((pieces, halo, LANES), lambda i: (0, jnp.minimum((i + 1) * (ts // halo), n_halo - 1), 0)),
            pl.BlockSpec((G, ts, ts + 2 * halo), lambda i: (0, 0, 0)),
            pl.BlockSpec((1, G, gd, gd), lambda i: (layer, 0, 0, 0)), vec,
            rows(5), rows(6), vec, vec, pl.BlockSpec((1, SG, P, P), lambda i: (layer, 0, 0, 0)),
            pl.BlockSpec((P, width), lambda i: (0, 0)),
            rows(7), pl.BlockSpec((n_mem, 2 * width), lambda i: (i // (seq // ts), 0)),
        ],
        out_specs=[out_rows, out_rows, out_rows],
        out_shape=[out, out, out],
        scratch_shapes=[pltpu.VMEM((ts + 2 * halo, width), BF16)],
        compiler_params=_params("parallel"), name="local_branches")(
            z, z, z, _pool_bands(ts), pool_w, _rows(pool_scale), z, z, _rows(sgu_ln_g), _rows(sgu_ln_b), sgu_w, bias,
            z, kv)


def _gate_merge_kernel(hb_ref, y0_ref, y1_ref, y2_ref, y3_ref, wg_ref, bg_ref, wb_ref, *refs, n_side):
    side_in, o_ref, side_out = refs[:n_side], refs[n_side], refs[n_side + 1:]
    acc = None
    for b, y_ref in enumerate((y0_ref, y1_ref, y2_ref, y3_ref)):
        gate = jax.nn.sigmoid(jnp.dot(hb_ref[...], wg_ref[b], preferred_element_type=F32) + bg_ref[0, b:b + 1, :])
        term = gate * jnp.dot(y_ref[...], wb_ref[b], preferred_element_type=F32)
        acc = term if acc is None else acc + term
    o_ref[...] = acc.astype(o_ref.dtype)
    _SideCasts.run(side_in, side_out)


def _gate_merge(hb, ys, w_gate, b_gate, w_branch, layer, side_weights=(), tm=512, tn=512):
    T, D = hb.shape
    bw = w_branch.shape[1]
    n_i = T // tm
    side = _SideCasts(side_weights, layer, (D // tn) * n_i, lambda j, i: j * n_i + i)
    yspec = pl.BlockSpec((tm, bw), lambda j, i: (i, 0))
    once = pl.Buffered(1)
    outs = pl.pallas_call(
        functools.partial(_gate_merge_kernel, n_side=len(side)), grid=(D // tn, n_i),
        in_specs=[pl.BlockSpec((tm, D), lambda j, i: (i, 0)), yspec, yspec, yspec, yspec,
                  pl.BlockSpec((N_BRANCHES, D, tn), lambda j, i: (0, 0, j), pipeline_mode=once),
                  pl.BlockSpec((1, N_BRANCHES, tn), lambda j, i: (layer, 0, j)),
                  pl.BlockSpec((N_BRANCHES, bw, tn), lambda j, i: (0, 0, j), pipeline_mode=once),
                  *side.in_specs],
        out_specs=[pl.BlockSpec((tm, tn), lambda j, i: (i, j)), *side.out_specs],
        out_shape=[jax.ShapeDtypeStruct((T, D), BF16), *side.out_shapes],
        compiler_params=_params("arbitrary", "arbitrary"), name="gate_merge")(
            hb, *ys, w_gate, b_gate, w_branch, *side.arrays)
    return outs[0], side.unflatten(outs[1:])


def _route(hb, wr_ref, br_ref, e_ref, gate_ref, rank_ref, cnt_ref, carry_ref, first):
    tm = hb.shape[0]

    @pl.when(first)
    def _():
        carry_ref[...] = jnp.zeros_like(carry_ref)

    nt_dims = (((1,), (1,)), ((), ()))
    logits = lax.dot_general(wr_ref[0], hb, nt_dims, preferred_element_type=F32) + br_ref[0]
    eidx = lax.broadcasted_iota(jnp.int32, logits.shape, 0)
    work = logits
    tops, hots = [], []
    for k in range(TOP_K):
        m = jnp.max(work, axis=0, keepdims=True)
        sel = jnp.min(jnp.where(work == m, eidx, N_EXPERTS), axis=0, keepdims=True)
        hot = eidx == sel
        work = jnp.where(hot, -jnp.inf, work)
        e_ref[k:k + 1, :] = sel
        tops.append(m)
        hots.append(hot)
    ex = [jnp.exp(t - tops[0]) for t in tops]
    den = ex[0] + ex[1] + ex[2] + ex[3]
    member = jnp.zeros(logits.shape, F32)
    for hot in hots:
        member = member + jnp.where(hot, 1.0, 0.0)
    tri = jnp.where(lax.broadcasted_iota(jnp.int32, (tm, tm), 0) <= lax.broadcasted_iota(jnp.int32, (tm, tm), 1),
                    1.0, 0.0).astype(BF16)
    incl = jnp.dot(member.astype(BF16), tri, preferred_element_type=F32)
    before = carry_ref[:, 0:1] + incl - member
    for k in range(TOP_K):
        gate_ref[k:k + 1, :] = ex[k] / den
        rank_ref[k:k + 1, :] = jnp.sum(jnp.where(hots[k], before, 0.0), axis=0, keepdims=True).astype(jnp.int32)
    carry_ref[...] = carry_ref[...] + incl[:, tm - 1:tm]
    cnt_ref[...] = carry_ref[...].astype(jnp.int32)


def _out_ln_kernel(m_ref, w_ref, h_ref, g_ref, b_ref, wr_ref, br_ref,
                   ho_ref, hpo_ref, e_ref, gate_ref, rank_ref, cnt_ref, carry_ref, hb_ref, *, alpha, n_tiles):
    tm = m_ref.shape[0]
    i = pl.program_id(0)
    slot = i % 2

    def project():
        mix = jnp.dot(m_ref[...], w_ref[...], preferred_element_type=F32)
        hn = _layer_norm(alpha * h_ref[...] + mix, g_ref[0], b_ref[0])
        ho_ref[...] = hn
        _store_row_major(hpo_ref, _pack_bf16_pairs(hn), tm)
        hb_ref[slot] = hn.astype(BF16)

    def route():
        _route(hb_ref[1 - slot], wr_ref, br_ref, e_ref, gate_ref, rank_ref, cnt_ref, carry_ref, first=i == 1)

    @pl.when(i < n_tiles)
    def _():
        project()

    @pl.when(i > 0)
    def _():
        route()


def _out_ln(merged, w_out, h, g, b, w_router_t, b_router, layer, alpha, tm=256):
    T, D = h.shape
    E = w_router_t.shape[1]
    pieces = _row_pitch(D // 2 // LANES)
    n_tiles = T // tm

    def cur(i):
        return (jnp.minimum(i, n_tiles - 1), 0)

    row = pl.BlockSpec((tm, D), cur)
    vec = pl.BlockSpec((1, 1, D), lambda i: (layer, 0, 0))
    tok = pl.BlockSpec((TOP_K, tm), lambda i: (0, jnp.maximum(i - 1, 0)))
    kern = functools.partial(_out_ln_kernel, alpha=alpha, n_tiles=n_tiles)
    return pl.pallas_call(
        kern, grid=(n_tiles + 1,),
        in_specs=[row, pl.BlockSpec((D, D), lambda i: (0, 0)), row, vec, vec,
                  pl.BlockSpec((1, E, D), lambda i: (layer, 0, 0)), pl.BlockSpec((1, E, 1), lambda i: (layer, 0, 0))],
        out_specs=[row, pl.BlockSpec((tm * pieces, LANES), cur), tok, tok, tok,
                   pl.BlockSpec((E, LANES), lambda i: (0, 0))],
        out_shape=[jax.ShapeDtypeStruct((T, D), F32), jax.ShapeDtypeStruct((T * pieces, LANES), U32),
                   jax.ShapeDtypeStruct((TOP_K, T), jnp.int32), jax.ShapeDtypeStruct((TOP_K, T), F32),
                   jax.ShapeDtypeStruct((TOP_K, T), jnp.int32), jax.ShapeDtypeStruct((E, LANES), jnp.int32)],
        scratch_shapes=[pltpu.VMEM((E, LANES), F32), pltpu.VMEM((2, tm, D), BF16)],
        compiler_params=_params("arbitrary"), name="out_ln1_router")(
            merged, w_out, h, _rows(g), _rows(b), w_router_t, b_router.reshape(b_router.shape[0], E, 1))


def _dispatch_kernel(plo_ref, phi_ref, dst_hbm, hp_ref, xs_hbm, idx_ref, zero_ref, idx_sem, row_sem, zero_sem,
                     *, n_tiles, n_blocks, n):
    assert hp_ref.shape[0] % n == 0 and zero_ref.shape[0] % n == 0
    tm = hp_ref.shape[0] // n
    blk_rows = zero_ref.shape[0]
    bm = blk_rows // n
    i = pl.program_id(0)

    def idx_copy(tile):
        return pltpu.make_async_copy(dst_hbm.at[tile], idx_ref, idx_sem)

    def zero_row(r):
        return pltpu.make_async_copy(zero_ref.at[pl.ds(0, n)], xs_hbm.at[pl.ds(r * n, n)], zero_sem)

    def zero_block(blk):
        return pltpu.make_async_copy(
            zero_ref, xs_hbm.at[pl.ds(pl.multiple_of(blk * blk_rows, 8), blk_rows)], zero_sem)

    def for_each(lo, hi, fn):
        def body(r, _):
            fn(r)
            return 0
        lax.fori_loop(lo, hi, body, 0)

    @pl.when(i == 0)
    def _():
        idx_copy(0).start()
        zero_ref[...] = jnp.zeros_like(zero_ref)
        for e in range(N_EXPERTS):
            for_each(plo_ref[e], phi_ref[e], lambda r: zero_row(r).start())
            for_each(plo_ref[e], phi_ref[e], lambda r: zero_row(r).wait())
        first_unused = phi_ref[N_EXPERTS - 1] // bm
        for_each(first_unused, n_blocks, lambda blk: zero_block(blk).start())
        for_each(first_unused, n_blocks, lambda blk: zero_block(blk).wait())

    idx_copy(i).wait()
    for r in range(tm):
        for k in range(TOP_K):
            d = idx_ref[r * TOP_K + k]
            pltpu.make_async_copy(hp_ref.at[pl.ds(r * n, n)], xs_hbm.at[pl.ds(d, n)], row_sem).start(priority=k % 2)

    @pl.when(i + 1 < n_tiles)
    def _():
        idx_copy(i + 1).start()

    for k in range(TOP_K):
        pltpu.make_async_copy(hp_ref, xs_hbm.at[pl.ds(0, tm * n)], row_sem).wait()


def _dispatch(hp, dest_tiles, pad_lo, pad_hi, n_slots, n):
    n_tiles, ktm = dest_tiles.shape
    tm = ktm // TOP_K
    kern = functools.partial(_dispatch_kernel, n_tiles=n_tiles, n_blocks=n_slots // MOE_BLOCK, n=n)
    grid_spec = pltpu.PrefetchScalarGridSpec(
        num_scalar_prefetch=2, grid=(n_tiles,),
        in_specs=[pl.BlockSpec(memory_space=pl.ANY), pl.BlockSpec((tm * n, LANES), lambda i, lo, hi: (i, 0))],
        out_specs=pl.BlockSpec(memory_space=pl.ANY),
        scratch_shapes=[pltpu.SMEM((ktm,), jnp.int32), pltpu.VMEM((MOE_BLOCK * n, LANES), U32),
                        pltpu.SemaphoreType.DMA, pltpu.SemaphoreType.DMA, pltpu.SemaphoreType.DMA])
    return pl.pallas_call(
        kern, grid_spec=grid_spec, out_shape=jax.ShapeDtypeStruct((n_slots * n, LANES), U32),
        compiler_params=_params("arbitrary"), name="moe_dispatch")(pad_lo, pad_hi, dest_tiles, hp)


def _moe_kernel(be_ref, bn_ref, x_ref, wup_ref, bup_ref, wdn_ref, bdn_ref, y_ref, *, bm):
    ed = wdn_ref.shape[1]
    n_x = wup_ref.shape[1] // 2 // LANES
    i = pl.program_id(0)

    def ffn(rows):
        x = _unpack_bf16_pairs(jnp.concatenate(_load_row_major(x_ref, rows, n_x), axis=1))
        gu = jnp.dot(x, wup_ref[0], preferred_element_type=F32) + bup_ref[0]
        glu = jnp.minimum(gu[:, :ed], SWIGLU_LIMIT)
        lin = jnp.clip(gu[:, ed:], -SWIGLU_LIMIT, SWIGLU_LIMIT)
        act = glu * jax.nn.sigmoid(SWIGLU_ALPHA * glu) * (lin + 1.0)
        y = jnp.dot(act.astype(BF16), wdn_ref[0], preferred_element_type=F32) + bdn_ref[0]
        _store_row_major(y_ref, _pack_bf16_pairs(y), rows)
        if rows < bm:
            rest = y_ref.shape[0] // bm * rows
            y_ref[rest:, :] = jnp.zeros((y_ref.shape[0] - rest, LANES), y_ref.dtype)

    @pl.when(bn_ref[i] > bm // 2)
    def _():
        ffn(bm)

    @pl.when(jnp.logical_and(bn_ref[i] > 0, bn_ref[i] <= bm // 2))
    def _():
        ffn(bm // 2)

    @pl.when(bn_ref[i] == 0)
    def _():
        y_ref[...] = jnp.zeros_like(y_ref)


def _moe(xs, blk_e, blk_n, w_up, b_up, w_down, b_down, layer):
    bm = MOE_BLOCK
    _, D, two_ed = w_up.shape
    ed = two_ed // 2
    n_x = _row_pitch(D // 2 // LANES)
    n_y = n_x
    n_blocks = xs.shape[0] // (bm * n_x)
    base = layer * N_EXPERTS

    def expert(i, be, bn):
        return (be[i], 0, 0)

    def expert_bias(i, be, bn):
        return (base + be[i], 0, 0)

    grid_spec = pltpu.PrefetchScalarGridSpec(
        num_scalar_prefetch=2, grid=(n_blocks,),
        in_specs=[pl.BlockSpec((bm * n_x, LANES), lambda i, be, bn: (jnp.where(bn[i] > 0, i, 0), 0)),
                  pl.BlockSpec((1, D, two_ed), expert), pl.BlockSpec((1, 1, two_ed), expert_bias),
                  pl.BlockSpec((1, ed, D), expert), pl.BlockSpec((1, 1, D), expert_bias)],
        out_specs=pl.BlockSpec((bm * n_y, LANES), lambda i, be, bn: (i, 0)))
    return pl.pallas_call(
        functools.partial(_moe_kernel, bm=bm), grid_spec=grid_spec,
        out_shape=jax.ShapeDtypeStruct((n_blocks * bm * n_y, LANES), U32),
        compiler_params=_params("arbitrary"), name="moe_experts")(blk_e, blk_n, xs, w_up, b_up, w_down, b_down)


def _combine_kernel(dst_hbm, y_hbm, gates_ref, h_ref, g_ref, b_ref, ho_ref, hbo_ref,
                    idx_ref, ybuf, gate_ref, r_ref, idx_sem, row_sem, *, n_tiles, alpha):
    tm, D = h_ref.shape
    n = D // 2 // LANES
    pitch = _row_pitch(n)
    per_seg = tm // n
    i = pl.program_id(0)
    slot = i % 2
    prev = 1 - slot

    def idx_copy(tile):
        return pltpu.make_async_copy(dst_hbm.at[tile], idx_ref, idx_sem)

    def issue(seg):
        for r in range(seg * per_seg, (seg + 1) * per_seg):
            for k in range(TOP_K):
                row = idx_ref[r * TOP_K + k]
                pltpu.make_async_copy(y_hbm.at[pl.ds(row, pitch)], ybuf.at[slot, k, pl.ds(r * pitch, pitch)],
                                      row_sem.at[slot]).start(priority=k % 2)

    def wait_rows():
        pltpu.make_async_copy(ybuf.at[prev], ybuf.at[prev], row_sem.at[prev]).wait()
        for k in range(TOP_K):
            gate_ref[k] = jnp.broadcast_to(gates_ref[:, k:k + 1], (tm, LANES))

    def piece(j):
        lo_cols = slice(j * LANES, (j + 1) * LANES)
        hi_cols = slice((n + j) * LANES, (n + j + 1) * LANES)
        lo = alpha * h_ref[:, lo_cols]
        hi = alpha * h_ref[:, hi_cols]
        for k in range(TOP_K):
            p = ybuf[prev, k, pl.ds(j, tm, stride=pitch), :]
            lo = lo + gate_ref[k] * lax.bitcast_convert_type(p << 16, F32)
            hi = hi + gate_ref[k] * lax.bitcast_convert_type(p & jnp.uint32(0xFFFF0000), F32)
        r_ref[:, lo_cols] = lo
        r_ref[:, hi_cols] = hi

    def finish():
        hn = _layer_norm(r_ref[...], g_ref[0], b_ref[0])
        ho_ref[...] = hn
        hbo_ref[...] = hn.astype(BF16)

    @pl.when(i == 0)
    def _():
        idx_copy(0).start()
        idx_copy(0).wait()
        for seg in range(n):
            issue(seg)

    @pl.when(jnp.logical_and(i > 0, i < n_tiles))
    def _():
        idx_copy(i).wait()
        wait_rows()
        for seg in range(n):
            issue(seg)
            piece(seg)

    @pl.when(i + 1 < n_tiles)
    def _():
        idx_copy(i + 1).start()

    @pl.when(i == n_tiles)
    def _():
        wait_rows()
        for j in range(n):
            piece(j)

    @pl.when(i > 0)
    def _():
        finish()


def _combine(dest_tiles, y_sorted, gates_t, h, g, b, layer, alpha):
    T, D = h.shape
    n_tiles, ktm = dest_tiles.shape
    tm = ktm // TOP_K

    def tile(i):
        return (jnp.maximum(i - 1, 0), 0)

    row = pl.BlockSpec((tm, D), tile)
    vec = pl.BlockSpec((1, 1, D), lambda i: (layer, 0, 0))
    kern = functools.partial(_combine_kernel, n_tiles=n_tiles, alpha=alpha)
    return pl.pallas_call(
        kern, grid=(n_tiles + 1,),
        in_specs=[pl.BlockSpec(memory_space=pl.ANY), pl.BlockSpec(memory_space=pl.ANY),
                  pl.BlockSpec((tm, TOP_K), tile), row, vec, vec],
        out_specs=[row, row],
        out_shape=[jax.ShapeDtypeStruct((T, D), F32), jax.ShapeDtypeStruct((T, D), BF16)],
        scratch_shapes=[pltpu.SMEM((ktm,), jnp.int32),
                        pltpu.VMEM((2, TOP_K, tm * _row_pitch(D // 2 // LANES), LANES), U32),
                        pltpu.VMEM((TOP_K, tm, LANES), F32), pltpu.VMEM((tm, D), F32),
                        pltpu.SemaphoreType.DMA, pltpu.SemaphoreType.DMA((2,))],
        compiler_params=_params("arbitrary"), name="moe_combine_ln2")(
            dest_tiles, y_sorted, gates_t, h, _rows(g), _rows(b))


def _routing_plan(top_e, rank, counts, n_blocks):
    T = top_e.shape[1]
    bm = MOE_BLOCK
    experts = jnp.arange(N_EXPERTS, dtype=jnp.int32)
    padded = (counts + bm - 1) // bm * bm
    pad_end = jnp.cumsum(padded)
    pad_start = pad_end - padded
    start_of = jnp.sum(jnp.where(top_e[None] == experts[:, None, None], pad_start[:, None, None], 0), axis=0)
    dest = start_of + rank
    blk_start = jnp.arange(n_blocks, dtype=jnp.int32) * bm
    blk_e = jnp.minimum(jnp.sum(pad_end[None, :] <= blk_start[:, None], axis=1), N_EXPERTS - 1).astype(jnp.int32)
    hot = blk_e[:, None] == experts[None, :]
    blk_cnt = jnp.sum(jnp.where(hot, counts[None, :], 0), axis=1)
    blk_pad_start = jnp.sum(jnp.where(hot, pad_start[None, :], 0), axis=1)
    blk_n = jnp.clip(blk_cnt - (blk_start - blk_pad_start), 0, bm).astype(jnp.int32)
    tm = ROUTE_TILE
    dest_tiles = dest.T.reshape(T // tm, tm * TOP_K).astype(jnp.int32)
    return dest_tiles, blk_e, blk_n, (pad_start + counts).astype(jnp.int32), pad_end.astype(jnp.int32)


def kernel(x, mem, ln_in_g, ln_in_b, w_in, pool_w, pool_scale, ret_log_gamma, ret_gn_g, sgu_ln_g, sgu_ln_b, sgu_w, sgu_b, w_mem_kv, w_branch, w_gate, b_gate, w_out, ln1_g, ln1_b, w_router, b_router, w_up, b_up, w_down, b_down, ln2_g, ln2_b):
    B, S, D = x.shape
    depth = w_in.shape[0]
    n_mem = mem.shape[1]
    T = B * S
    alpha = (2 * depth) ** 0.25
    head_dim = ret_gn_g.shape[1] // N_RET_HEADS
    n_blocks = -(-(T * TOP_K + N_EXPERTS * (MOE_BLOCK - 1)) // MOE_BLOCK)

    pos = jnp.arange(S, dtype=F32)
    inv_freq = jnp.exp(-math.log(ROPE_BASE) * jnp.arange(0, head_dim, 2, dtype=F32) / head_dim)
    ang = pos[:, None] * inv_freq[None, :]
    cos, sin = jnp.cos(ang), jnp.sin(ang)
    cos2 = jnp.concatenate([cos, cos], axis=1)
    sin2 = jnp.concatenate([-sin, sin], axis=1)

    pool_wb, sgu_wb = pool_w.astype(BF16), sgu_w.astype(BF16)
    w_router_t = jnp.swapaxes(w_router, 1, 2).astype(BF16)
    E = w_up.shape[1]
    b_up_r = b_up.reshape(depth * E, 1, -1)
    b_down_r = b_down.reshape(depth * E, 1, D)
    pitch_x = _row_pitch(D // 2 // LANES)
    pitch_y = pitch_x

    mem_b = mem.reshape(B * n_mem, D).astype(BF16)
    h, hb = _ln_in(x.reshape(T, D), ln_in_g, ln_in_b)
    for l in range(depth):
        z, (w_gate_b, w_branch_b, w_out_b) = _matmul(hb, w_in, l, tm=1024, tn=1024, name="in_proj",
                                                     side_weights=(w_gate, w_branch, w_out), piece_major=True)
        kv, _ = _matmul(mem_b, w_mem_kv, l, tm=B * n_mem, tn=1024, name="mem_kv")
        y_pool, y_sgu, y_mem = _local_branches(z, kv, pool_wb, pool_scale, sgu_ln_g, sgu_ln_b, sgu_wb, sgu_b,
                                               l, S, n_mem)
        ys = (y_pool, _retention(z, ret_log_gamma, ret_gn_g, cos2, sin2, l, B, S), y_sgu, y_mem)
        merged, (w_up_b, w_down_b) = _gate_merge(hb, ys, w_gate_b, b_gate, w_branch_b, l,
                                                 side_weights=(w_up, w_down))
        h, hp, top_e, gates, rank, cnt = _out_ln(merged, w_out_b, h, ln1_g, ln1_b, w_router_t, b_router, l, alpha)
        dest_tiles, blk_e, blk_n, pad_lo, pad_hi = _routing_plan(top_e, rank, cnt[:, 0], n_blocks)
        xs = _dispatch(hp, dest_tiles * pitch_x, pad_lo, pad_hi, n_blocks * MOE_BLOCK, pitch_x)
        y_sorted = _moe(xs, blk_e, blk_n, w_up_b, b_up_r, w_down_b, b_down_r, l)
        h, hb = _combine(dest_tiles * pitch_y, y_sorted, gates.T, h, ln2_g, ln2_b, l, alpha)
    return h.reshape(B, S, D)
```

```python
import functools
import math

import jax
import jax.numpy as jnp
from jax import lax
from jax.experimental import pallas as pl
from jax.experimental.pallas import tpu as pltpu

F32 = jnp.float32
BF16 = jnp.bfloat16
U32 = jnp.uint32

N_BRANCHES = 4
POOL_WINDOWS = (2, 4, 8, 16)
N_RET_HEADS = 8
RET_CHUNK = 128
ROPE_BASE = 10000.0
SGU_CHUNK = 128
N_SGU_GROUPS = 4
N_MEM_HEADS = 4
N_EXPERTS = 32
TOP_K = 4
SWIGLU_LIMIT = 7.0
SWIGLU_ALPHA = 1.702
LN_EPS = 1e-5

V7X_VMEM_BYTES = 64 * 1024 * 1024
VMEM_LIMIT_BYTES = V7X_VMEM_BYTES - 8 * 1024 * 1024
BF16_SUBLANE_TILE = 16
LANES = 128

MOE_BLOCK = 256
ROUTE_TILE = 128

def _params(*semantics):
    return pltpu.CompilerParams(dimension_semantics=semantics, vmem_limit_bytes=VMEM_LIMIT_BYTES)


def _rows(v):
    return v.reshape(v.shape[0], 1, v.shape[1])


def _layer_norm(x, g, b):
    mu = jnp.mean(x, axis=-1, keepdims=True)
    xc = x - mu
    var = jnp.mean(xc * xc, axis=-1, keepdims=True)
    return xc * lax.rsqrt(var + LN_EPS) * g + b


def _pack_bf16_pairs(x):
    n = x.shape[1] // 2
    xb = x.astype(BF16).astype(F32)
    lo = lax.bitcast_convert_type(xb[:, :n], U32)
    hi = lax.bitcast_convert_type(xb[:, n:], U32)
    return (lo >> 16) | hi


def _row_pitch(n):
    assert n % 2 == 0
    return n + 1


def _store_row_major(ref, x, rows):
    n = x.shape[1] // LANES
    pitch = _row_pitch(n)
    for j in range(n):
        ref[pl.ds(j, rows, stride=pitch), :] = x[:, j * LANES:(j + 1) * LANES]
    ref[pl.ds(n, rows, stride=pitch), :] = jnp.zeros((rows, LANES), ref.dtype)


def _load_row_major(ref, rows, n, index=()):
    return [ref[(*index, pl.ds(j, rows, stride=_row_pitch(n)), slice(None))] for j in range(n)]


def _unpack_bf16_pairs(p):
    lo = lax.bitcast_convert_type(p << 16, F32).astype(BF16)
    hi = lax.bitcast_convert_type(p & jnp.uint32(0xFFFF0000), F32).astype(BF16)
    return jnp.concatenate([lo, hi], axis=1)


def _ln_in_kernel(x_ref, g_ref, b_ref, h_ref, hb_ref):
    h = _layer_norm(x_ref[...], g_ref[...], b_ref[...])
    h_ref[...] = h
    hb_ref[...] = h.astype(BF16)


def _ln_in(x, g, b, tm=512):
    T, D = x.shape
    row = pl.BlockSpec((tm, D), lambda i: (i, 0))
    vec = pl.BlockSpec((1, D), lambda i: (0, 0))
    return pl.pallas_call(
        _ln_in_kernel, grid=(T // tm,), in_specs=[row, vec, vec], out_specs=[row, row],
        out_shape=[jax.ShapeDtypeStruct((T, D), F32), jax.ShapeDtypeStruct((T, D), BF16)],
        compiler_params=_params("parallel"), name="ln_in")(x, g.reshape(1, D), b.reshape(1, D))


class _SideCasts:
    def __init__(self, weights, layer, n_steps, step_of):
        self.arrays, self.in_specs, self.out_specs, self.out_shapes, self.shapes = [], [], [], [], []
        for w in weights:
            cols = w.shape[-1]
            rows = math.prod(w.shape[1:-1])
            slab = rows // n_steps
            assert slab * n_steps == rows and slab % BF16_SUBLANE_TILE == 0, (w.shape, n_steps)
            self.arrays.append(w.reshape(w.shape[0] * rows, cols))
            self.in_specs.append(pl.BlockSpec((slab, cols), lambda *g: (layer * n_steps + step_of(*g), 0)))
            self.out_specs.append(pl.BlockSpec((slab, cols), lambda *g: (step_of(*g), 0)))
            self.out_shapes.append(jax.ShapeDtypeStruct((rows, cols), BF16))
            self.shapes.append(w.shape[1:])

    def __len__(self):
        return len(self.arrays)

    @staticmethod
    def run(in_refs, out_refs):
        for i_ref, o_ref in zip(in_refs, out_refs):
            o_ref[...] = i_ref[...].astype(BF16)

    def unflatten(self, outs):
        return [o.reshape(s) for o, s in zip(outs, self.shapes)]


def _matmul_kernel(a_ref, w_ref, *refs, n_side):
    side_in, o_ref, side_out, wb_ref = refs[:n_side], refs[n_side], refs[n_side + 1:-1], refs[-1]

    @pl.when(pl.program_id(1) == 0)
    def _():
        wb_ref[...] = w_ref[0].astype(BF16)

    out = jnp.dot(a_ref[...], wb_ref[...], preferred_element_type=F32).astype(o_ref.dtype)
    if len(o_ref.shape) == 2:
        o_ref[...] = out
    else:
        for p in range(o_ref.shape[0]):
            o_ref[p] = out[:, p * LANES:(p + 1) * LANES]
    _SideCasts.run(side_in, side_out)


def _matmul(a, w, layer, tm, tn, name, side_weights=(), piece_major=False):
    M, K = a.shape
    N = w.shape[2]
    n_i = M // tm
    side = _SideCasts(side_weights, layer, (N // tn) * n_i, lambda j, i: j * n_i + i)
    if piece_major:
        out_spec = pl.BlockSpec((tn // LANES, tm, LANES), lambda j, i: (j, i, 0))
        out_shape = jax.ShapeDtypeStruct((N // LANES, M, LANES), BF16)
    else:
        out_spec = pl.BlockSpec((tm, tn), lambda j, i: (i, j))
        out_shape = jax.ShapeDtypeStruct((M, N), BF16)
    outs = pl.pallas_call(
        functools.partial(_matmul_kernel, n_side=len(side)), grid=(N // tn, n_i),
        in_specs=[pl.BlockSpec((tm, K), lambda j, i: (i, 0)), pl.BlockSpec((1, K, tn), lambda j, i: (layer, 0, j)),
                  *side.in_specs],
        out_specs=[out_spec, *side.out_specs],
        out_shape=[out_shape, *side.out_shapes],
        scratch_shapes=[pltpu.VMEM((K, tn), BF16)],
        compiler_params=_params("arbitrary", "arbitrary"), name=name)(a, w, *side.arrays)
    return outs[0], side.unflatten(outs[1:])


POOL_HALO = 64


def _pool_bands(ts):
    r = jnp.arange(ts)[:, None] + POOL_HALO
    c = jnp.arange(ts + 2 * POOL_HALO)[None, :]
    return jnp.stack([((c >= r - w // 2) & (c < r + w // 2)) for w in POOL_WINDOWS]).astype(BF16)


def _pool_kernel(z_ref, prev_ref, next_ref, band_ref, w_ref, scale_ref, o_ref, ext_ref, *, seq, ts):
    halo = POOL_HALO
    gd = w_ref.shape[2]
    pos0 = (pl.program_id(0) % (seq // ts)) * ts
    for p in range(z_ref.shape[0]):
        lanes = slice(p * LANES, (p + 1) * LANES)
        ext_ref[0:halo, lanes] = jnp.where(pos0 == 0, jnp.zeros_like(prev_ref[p]), prev_ref[p])
        ext_ref[halo:halo + ts, lanes] = z_ref[p]
        ext_ref[halo + ts:, lanes] = jnp.where(pos0 + ts == seq, jnp.zeros_like(next_ref[p]), next_ref[p])
    pos = pos0 + lax.broadcasted_iota(jnp.int32, (ts, gd), 0)
    for g, w in enumerate(POOL_WINDOWS):
        cols = slice(g * gd, (g + 1) * gd)
        half = w // 2
        acc = jnp.dot(band_ref[g], ext_ref[:, cols], preferred_element_type=F32)
        cnt = jnp.minimum(pos + half, seq) - jnp.maximum(pos - half, 0)
        mixed = acc / cnt.astype(F32) - ext_ref[halo:halo + ts, cols].astype(F32)
        y = jnp.dot(mixed.astype(BF16), w_ref[0, g], preferred_element_type=F32)
        o_ref[:, cols] = (y * scale_ref[0, :, cols]).astype(o_ref.dtype)


def _retention_kernel(lg_ref, q_ref, k_ref, v_ref, g_ref, cos_ref, sin_ref, gn_ref, o_ref,
                      rs_ref, kr_ref, kvf_ref, *, seq, unroll_bwd, unroll_fwd):
    C = RET_CHUNK
    d = q_ref.shape[2]
    nc = seq // C
    head = pl.program_id(0) % N_RET_HEADS
    lgf = lg_ref[0, head]
    lgb = lg_ref[1, head]
    row = lax.broadcasted_iota(jnp.int32, (C, C), 0).astype(F32)
    col = lax.broadcasted_iota(jnp.int32, (C, C), 1).astype(F32)
    diff = row - col
    decay = jnp.where(diff >= 0, jnp.exp(lgf * jnp.maximum(diff, 0.0)), jnp.exp(lgb * jnp.maximum(-diff, 0.0)))
    rowd = lax.broadcasted_iota(jnp.int32, (C, d), 0).astype(F32)
    qwf = jnp.exp(lgf * (rowd + 1.0))
    qwb = jnp.exp(lgb * (C - rowd))
    kwf = jnp.exp(lgf * (C - 1.0 - rowd))
    kwb = jnp.exp(lgb * rowd)
    chunk_decay_f = jnp.exp(lgf * jnp.full((d, d), float(C), F32))
    chunk_decay_b = jnp.exp(lgb * jnp.full((d, d), float(C), F32))
    tn_dims = (((0,), (0,)), ((), ()))
    nt_dims = (((1,), (1,)), ((), ()))

    def rotary(ref, sl):
        x = ref[0, sl, :].astype(F32)
        return x * cos_ref[sl, :] + pltpu.roll(x, d // 2, axis=1) * sin_ref[sl, :]

    def chunk(n):
        return pl.ds(pl.multiple_of(n * C, C), C)

    def bwd_body(j, state):
        n = nc - 1 - j
        sl = chunk(n)
        rs_ref[n] = state.astype(BF16)
        k = rotary(k_ref, sl) * (d ** -0.5)
        kr_ref[sl, :] = k.astype(BF16)
        kw = jnp.concatenate([k * kwf, k * kwb], axis=1).astype(BF16)
        kv = lax.dot_general(kw, v_ref[0, sl, :], tn_dims, preferred_element_type=F32)
        kvf_ref[n] = kv[:d]
        return state * chunk_decay_b + kv[d:]

    lax.fori_loop(0, nc, bwd_body, jnp.zeros((d, d), F32), unroll=unroll_bwd)

    def fwd_body(n, state):
        sl = chunk(n)
        qb = rotary(q_ref, sl).astype(BF16)
        v = v_ref[0, sl, :]
        scores = lax.dot_general(qb, kr_ref[sl, :], nt_dims, preferred_element_type=F32) * decay
        y = jnp.dot(scores.astype(BF16), v, preferred_element_type=F32)
        states = jnp.concatenate([state.astype(BF16), rs_ref[n]], axis=1)
        cross = jnp.dot(qb, states, preferred_element_type=F32)
        y = y + cross[:, :d] * qwf + cross[:, d:] * qwb
        mu = jnp.mean(y, axis=-1, keepdims=True)
        yc = y - mu
        var = jnp.mean(yc * yc, axis=-1, keepdims=True)
        yn = yc * lax.rsqrt(var + LN_EPS) * gn_ref[0]
        gate = g_ref[0, sl, :].astype(F32)
        o_ref[sl, :] = (yn * (gate * jax.nn.sigmoid(gate))).astype(o_ref.dtype)
        return state * chunk_decay_f + kvf_ref[n]

    lax.fori_loop(0, nc, fwd_body, jnp.zeros((d, d), F32), unroll=unroll_fwd)


def _retention(z, log_gamma, gn_g, cos2, sin2, layer, batch, seq, unroll_bwd=8, unroll_fwd=16):
    T = z.shape[1]
    H = N_RET_HEADS
    d = gn_g.shape[1] // H
    nc = seq // RET_CHUNK

    def col(split):
        return pl.BlockSpec((1, seq, d), lambda i: (split * H + i % H, i // H, 0))

    table = pl.BlockSpec((seq, d), lambda i: (0, 0))
    kern = functools.partial(_retention_kernel, seq=seq, unroll_bwd=unroll_bwd, unroll_fwd=unroll_fwd)
    return pl.pallas_call(
        kern, grid=(batch * H,),
        in_specs=[pl.BlockSpec(memory_space=pltpu.SMEM), col(1), col(2), col(3), col(4), table, table,
                  pl.BlockSpec((1, 1, d), lambda i: (layer, 0, i % H))],
        out_specs=pl.BlockSpec((seq, d), lambda i: (i // H, i % H)),
        out_shape=jax.ShapeDtypeStruct((T, H * d), BF16),
        scratch_shapes=[pltpu.VMEM((nc, d, d), BF16), pltpu.VMEM((seq, d), BF16), pltpu.VMEM((nc, d, d), F32)],
        compiler_params=_params("parallel"), name="retention")(
            log_gamma[layer], z, z, z, z, cos2, sin2, _rows(gn_g))


def _sgu_kernel(u_ref, v_ref, lng_ref, lnb_ref, w_ref, bias_ref, o_ref, *, chunks):
    P = SGU_CHUNK
    gd = o_ref.shape[1] // N_SGU_GROUPS

    def body(c, _):
        rows = pl.ds(pl.multiple_of(c * P, P), P)
        v = jnp.concatenate([v_ref[p, rows, :] for p in range(v_ref.shape[0])], axis=1)
        vn = _layer_norm(jax.nn.gelu(v.astype(F32)), lng_ref[0], lnb_ref[0]).astype(BF16)
        for g in range(N_SGU_GROUPS):
            cols = slice(g * gd, (g + 1) * gd)
            mixed = jnp.dot(w_ref[0, g], vn[:, cols], preferred_element_type=F32) + bias_ref[:, cols]
            ppg = gd // LANES
            u = jnp.concatenate([u_ref[p, rows, :] for p in range(g * ppg, (g + 1) * ppg)], axis=1)
            u = jax.nn.gelu(u.astype(F32))
            o_ref[rows, cols] = (u * mixed).astype(o_ref.dtype)
        return 0

    lax.fori_loop(0, chunks, body, 0, unroll=True)


def _mem_attn_kernel(q_ref, kv_ref, o_ref):
    width = o_ref.shape[1]
    hd = width // N_MEM_HEADS
    nt_dims = (((1,), (1,)), ((), ()))
    for h in range(N_MEM_HEADS):
        cols = slice(h * hd, (h + 1) * hd)
        pph = hd // LANES
        q = jnp.concatenate([q_ref[p] for p in range(h * pph, (h + 1) * pph)], axis=1)
        s = lax.dot_general(q, kv_ref[:, cols], nt_dims, preferred_element_type=F32) * (hd ** -0.5)
        p = jnp.exp(s - jnp.max(s, axis=-1, keepdims=True))
        p = p / jnp.sum(p, axis=-1, keepdims=True)
        v = kv_ref[:, width + h * hd:width + (h + 1) * hd]
        o_ref[:, cols] = jnp.dot(p.astype(BF16), v, preferred_element_type=F32).astype(o_ref.dtype)


def _local_branches_kernel(a_ref, prev_ref, next_ref, band_ref, pw_ref, ps_ref, u_ref, v_ref, lng_ref, lnb_ref,
                           sw_ref, bias_ref, q_ref, kv_ref, pool_ref, sgu_ref, mem_ref, ext_ref, *, seq, ts):
    _pool_kernel(a_ref, prev_ref, next_ref, band_ref, pw_ref, ps_ref, pool_ref, ext_ref, seq=seq, ts=ts)
    _sgu_kernel(u_ref, v_ref, lng_ref, lnb_ref, sw_ref, bias_ref, sgu_ref, chunks=ts // SGU_CHUNK)
    _mem_attn_kernel(q_ref, kv_ref, mem_ref)


def _local_branches(z, kv, pool_w, pool_scale, sgu_ln_g, sgu_ln_b, sgu_w, sgu_b, layer, seq, n_mem, ts=512):
    T = z.shape[1]
    _, G, gd, _ = pool_w.shape
    width = G * gd
    pieces = width // LANES
    _, SG, P, _ = sgu_w.shape
    halo = POOL_HALO
    n_halo = T // halo
    bias = jnp.repeat(sgu_b[layer].T.astype(F32), width // SG, axis=1)
    kern = functools.partial(_local_branches_kernel, seq=seq, ts=ts)

    def rows(split):
        return pl.BlockSpec((pieces, ts, LANES), lambda i: (split, i, 0))

    vec = pl.BlockSpec((1, 1, width), lambda i: (layer, 0, 0))
    out = jax.ShapeDtypeStruct((T, width), BF16)
    out_rows = pl.BlockSpec((ts, width), lambda i: (i, 0))
    return pl.pallas_call(
        kern, grid=(T // ts,),
        in_specs=[
            rows(0),
            pl.BlockSpec((pieces, halo, LANES), lambda i: (0, jnp.maximum(i * (ts // halo) - 1, 0), 0)),
            pl.BlockSpec((pieces, halo, LANES), lambda i: (0, jnp.minimum((i + 1) * (ts // halo), n_halo - 1), 0)),
            pl.BlockSpec((G, ts, ts + 2 * halo), lambda i: (0, 0, 0)),
            pl.BlockSpec((1, G, gd, gd), lambda i: (layer, 0, 0, 0)), vec,
            rows(5), rows(6), vec, vec, pl.BlockSpec((1, SG, P, P), lambda i: (layer, 0, 0, 0)),
            pl.BlockSpec((P, width), lambda i: (0, 0)),
            rows(7), pl.BlockSpec((n_mem, 2 * width), lambda i: (i // (seq // ts), 0)),
        ],
        out_specs=[out_rows, out_rows, out_rows],
        out_shape=[out, out, out],
        scratch_shapes=[pltpu.VMEM((ts + 2 * halo, width), BF16)],
        compiler_params=_params("parallel"), name="local_branches")(
            z, z, z, _pool_bands(ts), pool_w, _rows(pool_scale), z, z, _rows(sgu_ln_g), _rows(sgu_ln_b), sgu_w, bias,
            z, kv)


def _gate_merge_kernel(hb_ref, y0_ref, y1_ref, y2_ref, y3_ref, wg_ref, bg_ref, wb_ref, *refs, n_side):
    side_in, o_ref, side_out = refs[:n_side], refs[n_side], refs[n_side + 1:]
    acc = None
    for b, y_ref in enumerate((y0_ref, y1_ref, y2_ref, y3_ref)):
        gate = jax.nn.sigmoid(jnp.dot(hb_ref[...], wg_ref[b], preferred_element_type=F32) + bg_ref[0, b:b + 1, :])
        term = gate * jnp.dot(y_ref[...], wb_ref[b], preferred_element_type=F32)
        acc = term if acc is None else acc + term
    o_ref[...] = acc.astype(o_ref.dtype)
    _SideCasts.run(side_in, side_out)


def _gate_merge(hb, ys, w_gate, b_gate, w_branch, layer, side_weights=(), tm=512, tn=512):
    T, D = hb.shape
    bw = w_branch.shape[1]
    n_i = T // tm
    side = _SideCasts(side_weights, layer, (D // tn) * n_i, lambda j, i: j * n_i + i)
    yspec = pl.BlockSpec((tm, bw), lambda j, i: (i, 0))
    once = pl.Buffered(1)
    outs = pl.pallas_call(
        functools.partial(_gate_merge_kernel, n_side=len(side)), grid=(D // tn, n_i),
        in_specs=[pl.BlockSpec((tm, D), lambda j, i: (i, 0)), yspec, yspec, yspec, yspec,
                  pl.BlockSpec((N_BRANCHES, D, tn), lambda j, i: (0, 0, j), pipeline_mode=once),
                  pl.BlockSpec((1, N_BRANCHES, tn), lambda j, i: (layer, 0, j)),
                  pl.BlockSpec((N_BRANCHES, bw, tn), lambda j, i: (0, 0, j), pipeline_mode=once),
                  *side.in_specs],
        out_specs=[pl.BlockSpec((tm, tn), lambda j, i: (i, j)), *side.out_specs],
        out_shape=[jax.ShapeDtypeStruct((T, D), BF16), *side.out_shapes],
        compiler_params=_params("arbitrary", "arbitrary"), name="gate_merge")(
            hb, *ys, w_gate, b_gate, w_branch, *side.arrays)
    return outs[0], side.unflatten(outs[1:])


def _route(hb, wr_ref, br_ref, e_ref, gate_ref, rank_ref, cnt_ref, carry_ref, first):
    tm = hb.shape[0]

    @pl.when(first)
    def _():
        carry_ref[...] = jnp.zeros_like(carry_ref)

    nt_dims = (((1,), (1,)), ((), ()))
    logits = lax.dot_general(wr_ref[0], hb, nt_dims, preferred_element_type=F32) + br_ref[0]
    eidx = lax.broadcasted_iota(jnp.int32, logits.shape, 0)
    work = logits
    tops, hots = [], []
    for k in range(TOP_K):
        m = jnp.max(work, axis=0, keepdims=True)
        sel = jnp.min(jnp.where(work == m, eidx, N_EXPERTS), axis=0, keepdims=True)
        hot = eidx == sel
        work = jnp.where(hot, -jnp.inf, work)
        e_ref[k:k + 1, :] = sel
        tops.append(m)
        hots.append(hot)
    ex = [jnp.exp(t - tops[0]) for t in tops]
    den = ex[0] + ex[1] + ex[2] + ex[3]
    member = jnp.zeros(logits.shape, F32)
    for hot in hots:
        member = member + jnp.where(hot, 1.0, 0.0)
    tri = jnp.where(lax.broadcasted_iota(jnp.int32, (tm, tm), 0) <= lax.broadcasted_iota(jnp.int32, (tm, tm), 1),
                    1.0, 0.0).astype(BF16)
    incl = jnp.dot(member.astype(BF16), tri, preferred_element_type=F32)
    before = carry_ref[:, 0:1] + incl - member
    for k in range(TOP_K):
        gate_ref[k:k + 1, :] = ex[k] / den
        rank_ref[k:k + 1, :] = jnp.sum(jnp.where(hots[k], before, 0.0), axis=0, keepdims=True).astype(jnp.int32)
    carry_ref[...] = carry_ref[...] + incl[:, tm - 1:tm]
    cnt_ref[...] = carry_ref[...].astype(jnp.int32)


def _out_ln_kernel(m_ref, w_ref, h_ref, g_ref, b_ref, wr_ref, br_ref,
                   ho_ref, hpo_ref, e_ref, gate_ref, rank_ref, cnt_ref, carry_ref, hb_ref, *, alpha, n_tiles):
    tm = m_ref.shape[0]
    i = pl.program_id(0)
    slot = i % 2

    def project():
        mix = jnp.dot(m_ref[...], w_ref[...], preferred_element_type=F32)
        hn = _layer_norm(alpha * h_ref[...] + mix, g_ref[0], b_ref[0])
        ho_ref[...] = hn
        _store_row_major(hpo_ref, _pack_bf16_pairs(hn), tm)
        hb_ref[slot] = hn.astype(BF16)

    def route():
        _route(hb_ref[1 - slot], wr_ref, br_ref, e_ref, gate_ref, rank_ref, cnt_ref, carry_ref, first=i == 1)

    @pl.when(i < n_tiles)
    def _():
        project()

    @pl.when(i > 0)
    def _():
        route()


def _out_ln(merged, w_out, h, g, b, w_router_t, b_router, layer, alpha, tm=256):
    T, D = h.shape
    E = w_router_t.shape[1]
    pieces = _row_pitch(D // 2 // LANES)
    n_tiles = T // tm

    def cur(i):
        return (jnp.minimum(i, n_tiles - 1), 0)

    row = pl.BlockSpec((tm, D), cur)
    vec = pl.BlockSpec((1, 1, D), lambda i: (layer, 0, 0))
    tok = pl.BlockSpec((TOP_K, tm), lambda i: (0, jnp.maximum(i - 1, 0)))
    kern = functools.partial(_out_ln_kernel, alpha=alpha, n_tiles=n_tiles)
    return pl.pallas_call(
        kern, grid=(n_tiles + 1,),
        in_specs=[row, pl.BlockSpec((D, D), lambda i: (0, 0)), row, vec, vec,
                  pl.BlockSpec((1, E, D), lambda i: (layer, 0, 0)), pl.BlockSpec((1, E, 1), lambda i: (layer, 0, 0))],
        out_specs=[row, pl.BlockSpec((tm * pieces, LANES), cur), tok, tok, tok,
                   pl.BlockSpec((E, LANES), lambda i: (0, 0))],
        out_shape=[jax.ShapeDtypeStruct((T, D), F32), jax.ShapeDtypeStruct((T * pieces, LANES), U32),
                   jax.ShapeDtypeStruct((TOP_K, T), jnp.int32), jax.ShapeDtypeStruct((TOP_K, T), F32),
                   jax.ShapeDtypeStruct((TOP_K, T), jnp.int32), jax.ShapeDtypeStruct((E, LANES), jnp.int32)],
        scratch_shapes=[pltpu.VMEM((E, LANES), F32), pltpu.VMEM((2, tm, D), BF16)],
        compiler_params=_params("arbitrary"), name="out_ln1_router")(
            merged, w_out, h, _rows(g), _rows(b), w_router_t, b_router.reshape(b_router.shape[0], E, 1))


def _dispatch_kernel(plo_ref, phi_ref, dst_hbm, hp_ref, xs_hbm, idx_ref, zero_ref, idx_sem, row_sem, zero_sem,
                     *, n_tiles, n_blocks, n, tm):
    hp_hbm = hp_ref
    assert zero_ref.shape[0] % n == 0
    blk_rows = zero_ref.shape[0]
    bm = blk_rows // n
    i = pl.program_id(0)
    slot = i % 2

    def wait_rows(s):
        for k in range(TOP_K):
            pltpu.make_async_copy(xs_hbm.at[pl.ds(0, tm * n)], xs_hbm.at[pl.ds(0, tm * n)], row_sem.at[s]).wait()

    def idx_copy(tile):
        return pltpu.make_async_copy(dst_hbm.at[tile], idx_ref, idx_sem)

    def zero_row(r):
        return pltpu.make_async_copy(zero_ref.at[pl.ds(0, n)], xs_hbm.at[pl.ds(r * n, n)], zero_sem)

    def zero_block(blk):
        return pltpu.make_async_copy(
            zero_ref, xs_hbm.at[pl.ds(pl.multiple_of(blk * blk_rows, 8), blk_rows)], zero_sem)

    def for_each(lo, hi, fn):
        def body(r, _):
            fn(r)
            return 0
        lax.fori_loop(lo, hi, body, 0)

    @pl.when(i == 0)
    def _():
        idx_copy(0).start()
        zero_ref[...] = jnp.zeros_like(zero_ref)
        for e in range(N_EXPERTS):
            for_each(plo_ref[e], phi_ref[e], lambda r: zero_row(r).start())
            for_each(plo_ref[e], phi_ref[e], lambda r: zero_row(r).wait())
        first_unused = phi_ref[N_EXPERTS - 1] // bm
        for_each(first_unused, n_blocks, lambda blk: zero_block(blk).start())
        for_each(first_unused, n_blocks, lambda blk: zero_block(blk).wait())

    idx_copy(i).wait()
    base = i * (tm * n)
    for r in range(tm):
        for k in range(TOP_K):
            d = idx_ref[r * TOP_K + k]
            pltpu.make_async_copy(hp_hbm.at[pl.ds(base + r * n, n)], xs_hbm.at[pl.ds(d, n)],
                                  row_sem.at[slot]).start(priority=k % 2)

    @pl.when(i + 1 < n_tiles)
    def _():
        idx_copy(i + 1).start()

    @pl.when(i > 0)
    def _():
        wait_rows(1 - slot)

    @pl.when(i == n_tiles - 1)
    def _():
        wait_rows(slot)


def _dispatch(hp, dest_tiles, pad_lo, pad_hi, n_slots, n):
    n_tiles, ktm = dest_tiles.shape
    tm = ktm // TOP_K
    kern = functools.partial(_dispatch_kernel, n_tiles=n_tiles, n_blocks=n_slots // MOE_BLOCK, n=n, tm=tm)
    grid_spec = pltpu.PrefetchScalarGridSpec(
        num_scalar_prefetch=2, grid=(n_tiles,),
        in_specs=[pl.BlockSpec(memory_space=pl.ANY), pl.BlockSpec(memory_space=pl.ANY)],
        out_specs=pl.BlockSpec(memory_space=pl.ANY),
        scratch_shapes=[pltpu.SMEM((ktm,), jnp.int32), pltpu.VMEM((MOE_BLOCK * n, LANES), U32),
                        pltpu.SemaphoreType.DMA, pltpu.SemaphoreType.DMA((2,)), pltpu.SemaphoreType.DMA])
    return pl.pallas_call(
        kern, grid_spec=grid_spec, out_shape=jax.ShapeDtypeStruct((n_slots * n, LANES), U32),
        compiler_params=_params("arbitrary"), name="moe_dispatch")(pad_lo, pad_hi, dest_tiles, hp)


def _moe_kernel(be_ref, bn_ref, x_ref, wup_ref, bup_ref, wdn_ref, bdn_ref, y_ref, *, bm):
    ed = wdn_ref.shape[1]
    n_x = wup_ref.shape[1] // 2 // LANES
    i = pl.program_id(0)

    def ffn(rows):
        x = _unpack_bf16_pairs(jnp.concatenate(_load_row_major(x_ref, rows, n_x), axis=1))
        gu = jnp.dot(x, wup_ref[0], preferred_element_type=F32) + bup_ref[0]
        glu = jnp.minimum(gu[:, :ed], SWIGLU_LIMIT)
        lin = jnp.clip(gu[:, ed:], -SWIGLU_LIMIT, SWIGLU_LIMIT)
        act = glu * jax.nn.sigmoid(SWIGLU_ALPHA * glu) * (lin + 1.0)
        y = jnp.dot(act.astype(BF16), wdn_ref[0], preferred_element_type=F32) + bdn_ref[0]
        _store_row_major(y_ref, _pack_bf16_pairs(y), rows)
        if rows < bm:
            rest = y_ref.shape[0] // bm * rows
            y_ref[rest:, :] = jnp.zeros((y_ref.shape[0] - rest, LANES), y_ref.dtype)

    @pl.when(bn_ref[i] > bm // 2)
    def _():
        ffn(bm)

    @pl.when(jnp.logical_and(bn_ref[i] > 0, bn_ref[i] <= bm // 2))
    def _():
        ffn(bm // 2)

    @pl.when(bn_ref[i] == 0)
    def _():
        y_ref[...] = jnp.zeros_like(y_ref)


def _moe(xs, blk_e, blk_n, w_up, b_up, w_down, b_down, layer):
    bm = MOE_BLOCK
    _, D, two_ed = w_up.shape
    ed = two_ed // 2
    n_x = _row_pitch(D // 2 // LANES)
    n_y = n_x
    n_blocks = xs.shape[0] // (bm * n_x)
    base = layer * N_EXPERTS

    def expert(i, be, bn):
        return (be[i], 0, 0)

    def expert_bias(i, be, bn):
        return (base + be[i], 0, 0)

    grid_spec = pltpu.PrefetchScalarGridSpec(
        num_scalar_prefetch=2, grid=(n_blocks,),
        in_specs=[pl.BlockSpec((bm * n_x, LANES), lambda i, be, bn: (jnp.where(bn[i] > 0, i, 0), 0)),
                  pl.BlockSpec((1, D, two_ed), expert), pl.BlockSpec((1, 1, two_ed), expert_bias),
                  pl.BlockSpec((1, ed, D), expert), pl.BlockSpec((1, 1, D), expert_bias)],
        out_specs=pl.BlockSpec((bm * n_y, LANES), lambda i, be, bn: (i, 0)))
    return pl.pallas_call(
        functools.partial(_moe_kernel, bm=bm), grid_spec=grid_spec,
        out_shape=jax.ShapeDtypeStruct((n_blocks * bm * n_y, LANES), U32),
        compiler_params=_params("arbitrary"), name="moe_experts")(blk_e, blk_n, xs, w_up, b_up, w_down, b_down)


def _combine_kernel(dst_hbm, y_hbm, gates_ref, h_ref, g_ref, b_ref, ho_ref, hbo_ref,
                    idx_ref, ybuf, gate_ref, r_ref, idx_sem, row_sem, *, n_tiles, alpha):
    tm, D = h_ref.shape
    n = D // 2 // LANES
    pitch = _row_pitch(n)
    per_seg = tm // n
    i = pl.program_id(0)
    slot = i % 2
    prev = 1 - slot

    def idx_copy(tile):
        return pltpu.make_async_copy(dst_hbm.at[tile], idx_ref, idx_sem)

    def issue(seg):
        for r in range(seg * per_seg, (seg + 1) * per_seg):
            for k in range(TOP_K):
                row = idx_ref[r * TOP_K + k]
                pltpu.make_async_copy(y_hbm.at[pl.ds(row, pitch)], ybuf.at[slot, k, pl.ds(r * pitch, pitch)],
                                      row_sem.at[slot]).start(priority=k % 2)

    def wait_rows():
        pltpu.make_async_copy(ybuf.at[prev], ybuf.at[prev], row_sem.at[prev]).wait()
        for k in range(TOP_K):
            gate_ref[k] = jnp.broadcast_to(gates_ref[:, k:k + 1], (tm, LANES))

    def piece(j):
        lo_cols = slice(j * LANES, (j + 1) * LANES)
        hi_cols = slice((n + j) * LANES, (n + j + 1) * LANES)
        lo = alpha * h_ref[:, lo_cols]
        hi = alpha * h_ref[:, hi_cols]
        for k in range(TOP_K):
            p = ybuf[prev, k, pl.ds(j, tm, stride=pitch), :]
            lo = lo + gate_ref[k] * lax.bitcast_convert_type(p << 16, F32)
            hi = hi + gate_ref[k] * lax.bitcast_convert_type(p & jnp.uint32(0xFFFF0000), F32)
        r_ref[:, lo_cols] = lo
        r_ref[:, hi_cols] = hi

    def finish():
        hn = _layer_norm(r_ref[...], g_ref[0], b_ref[0])
        ho_ref[...] = hn
        hbo_ref[...] = hn.astype(BF16)

    @pl.when(i == 0)
    def _():
        idx_copy(0).start()
        idx_copy(0).wait()
        for seg in range(n):
            issue(seg)

    @pl.when(jnp.logical_and(i > 0, i < n_tiles))
    def _():
        idx_copy(i).wait()
        wait_rows()
        for seg in range(n):
            issue(seg)
            piece(seg)

    @pl.when(i + 1 < n_tiles)
    def _():
        idx_copy(i + 1).start()

    @pl.when(i == n_tiles)
    def _():
        wait_rows()
        for j in range(n):
            piece(j)

    @pl.when(i > 0)
    def _():
        finish()


def _combine(dest_tiles, y_sorted, gates_t, h, g, b, layer, alpha):
    T, D = h.shape
    n_tiles, ktm = dest_tiles.shape
    tm = ktm // TOP_K

    def tile(i):
        return (jnp.maximum(i - 1, 0), 0)

    row = pl.BlockSpec((tm, D), tile)
    vec = pl.BlockSpec((1, 1, D), lambda i: (layer, 0, 0))
    kern = functools.partial(_combine_kernel, n_tiles=n_tiles, alpha=alpha)
    return pl.pallas_call(
        kern, grid=(n_tiles + 1,),
        in_specs=[pl.BlockSpec(memory_space=pl.ANY), pl.BlockSpec(memory_space=pl.ANY),
                  pl.BlockSpec((tm, TOP_K), tile), row, vec, vec],
        out_specs=[row, row],
        out_shape=[jax.ShapeDtypeStruct((T, D), F32), jax.ShapeDtypeStruct((T, D), BF16)],
        scratch_shapes=[pltpu.SMEM((ktm,), jnp.int32),
                        pltpu.VMEM((2, TOP_K, tm * _row_pitch(D // 2 // LANES), LANES), U32),
                        pltpu.VMEM((TOP_K, tm, LANES), F32), pltpu.VMEM((tm, D), F32),
                        pltpu.SemaphoreType.DMA, pltpu.SemaphoreType.DMA((2,))],
        compiler_params=_params("arbitrary"), name="moe_combine_ln2")(
            dest_tiles, y_sorted, gates_t, h, _rows(g), _rows(b))


def _routing_plan(top_e, rank, counts, n_blocks):
    T = top_e.shape[1]
    bm = MOE_BLOCK
    experts = jnp.arange(N_EXPERTS, dtype=jnp.int32)
    padded = (counts + bm - 1) // bm * bm
    pad_end = jnp.cumsum(padded)
    pad_start = pad_end - padded
    start_of = jnp.sum(jnp.where(top_e[None] == experts[:, None, None], pad_start[:, None, None], 0), axis=0)
    dest = start_of + rank
    blk_start = jnp.arange(n_blocks, dtype=jnp.int32) * bm
    blk_e = jnp.minimum(jnp.sum(pad_end[None, :] <= blk_start[:, None], axis=1), N_EXPERTS - 1).astype(jnp.int32)
    hot = blk_e[:, None] == experts[None, :]
    blk_cnt = jnp.sum(jnp.where(hot, counts[None, :], 0), axis=1)
    blk_pad_start = jnp.sum(jnp.where(hot, pad_start[None, :], 0), axis=1)
    blk_n = jnp.clip(blk_cnt - (blk_start - blk_pad_start), 0, bm).astype(jnp.int32)
    tm = ROUTE_TILE
    dest_tiles = dest.T.reshape(T // tm, tm * TOP_K).astype(jnp.int32)
    return dest_tiles, blk_e, blk_n, (pad_start + counts).astype(jnp.int32), pad_end.astype(jnp.int32)


def kernel(x, mem, ln_in_g, ln_in_b, w_in, pool_w, pool_scale, ret_log_gamma, ret_gn_g, sgu_ln_g, sgu_ln_b, sgu_w, sgu_b, w_mem_kv, w_branch, w_gate, b_gate, w_out, ln1_g, ln1_b, w_router, b_router, w_up, b_up, w_down, b_down, ln2_g, ln2_b):
    B, S, D = x.shape
    depth = w_in.shape[0]
    n_mem = mem.shape[1]
    T = B * S
    alpha = (2 * depth) ** 0.25
    head_dim = ret_gn_g.shape[1] // N_RET_HEADS
    n_blocks = -(-(T * TOP_K + N_EXPERTS * (MOE_BLOCK - 1)) // MOE_BLOCK)

    pos = jnp.arange(S, dtype=F32)
    inv_freq = jnp.exp(-math.log(ROPE_BASE) * jnp.arange(0, head_dim, 2, dtype=F32) / head_dim)
    ang = pos[:, None] * inv_freq[None, :]
    cos, sin = jnp.cos(ang), jnp.sin(ang)
    cos2 = jnp.concatenate([cos, cos], axis=1)
    sin2 = jnp.concatenate([-sin, sin], axis=1)

    pool_wb, sgu_wb = pool_w.astype(BF16), sgu_w.astype(BF16)
    w_router_t = jnp.swapaxes(w_router, 1, 2).astype(BF16)
    E = w_up.shape[1]
    b_up_r = b_up.reshape(depth * E, 1, -1)
    b_down_r = b_down.reshape(depth * E, 1, D)
    pitch_x = _row_pitch(D // 2 // LANES)
    pitch_y = pitch_x

    mem_b = mem.reshape(B * n_mem, D).astype(BF16)
    h, hb = _ln_in(x.reshape(T, D), ln_in_g, ln_in_b)
    for l in range(depth):
        z, (w_gate_b, w_branch_b, w_out_b) = _matmul(hb, w_in, l, tm=1024, tn=1024, name="in_proj",
                                                     side_weights=(w_gate, w_branch, w_out), piece_major=True)
        kv, _ = _matmul(mem_b, w_mem_kv, l, tm=B * n_mem, tn=1024, name="mem_kv")
        y_pool, y_sgu, y_mem = _local_branches(z, kv, pool_wb, pool_scale, sgu_ln_g, sgu_ln_b, sgu_wb, sgu_b,
                                               l, S, n_mem)
        ys = (y_pool, _retention(z, ret_log_gamma, ret_gn_g, cos2, sin2, l, B, S), y_sgu, y_mem)
        merged, (w_up_b, w_down_b) = _gate_merge(hb, ys, w_gate_b, b_gate, w_branch_b, l,
                                                 side_weights=(w_up, w_down))
        h, hp, top_e, gates, rank, cnt = _out_ln(merged, w_out_b, h, ln1_g, ln1_b, w_router_t, b_router, l, alpha)
        dest_tiles, blk_e, blk_n, pad_lo, pad_hi = _routing_plan(top_e, rank, cnt[:, 0], n_blocks)
        xs = _dispatch(hp, dest_tiles * pitch_x, pad_lo, pad_hi, n_blocks * MOE_BLOCK, pitch_x)
        y_sorted = _moe(xs, blk_e, blk_n, w_up_b, b_up_r, w_down_b, b_down_r, l)
        h, hb = _combine(dest_tiles * pitch_y, y_sorted, gates.T, h, ln2_g, ln2_b, l, alpha)
    return h.reshape(B, S, D)
```
